```python
import math
import jax
import jax.numpy as jnp
from jax import lax
import numpy as np

D_MODEL = 2048
BATCH = 2
SEQ = 4096
DEPTH = 4
DEC_BATCH = 8
DEC_SEQ = 8
PAST_LEN = 16384
PAGE_SIZE = 128

N_A_LAYERS = DEPTH // 2
N_B_LAYERS = DEPTH - N_A_LAYERS
CONV_WIDTH = 3
N_HEADS = 16
HEAD_DIM = D_MODEL // N_HEADS // 2
V_DIM = 2 * HEAD_DIM
QK_WIDTH = N_HEADS * 2 * HEAD_DIM
ATTN_SCALE = HEAD_DIM ** -0.5
Q_BLOCK = 128
NUM_BUCKETS = 32
MAX_DISTANCE = 128
N_EXPERTS = 32
TOP_K = 4
D_FF = D_MODEL
SWIGLU_LIMIT = 7.0
SWIGLU_ALPHA = 1.702
LN_EPS = 1e-5
RMS_EPS = 1e-5
ALPHA = (2 * DEPTH) ** 0.25
BETA = (8 * DEPTH) ** -0.25

kernel_name = "yoco_shortconv_diffattn_moe_step"


def layer_norm(x, g, b):
    xf = x.astype(jnp.float32)
    mu = jnp.mean(xf, axis=-1, keepdims=True)
    xc = xf - mu
    var = jnp.mean(xc * xc, axis=-1, keepdims=True)
    return (xc * lax.rsqrt(var + LN_EPS) * g.astype(jnp.float32) + b.astype(jnp.float32)).astype(x.dtype)


def short_conv_mixer(x, conv_state, w_in, w_conv, w_out):
    L = x.shape[1]
    bch = jnp.einsum('bld,de->ble', x, w_in)
    b_gate, c_gate, h = jnp.split(bch, 3, axis=-1)
    u = c_gate * h
    u_ext = jnp.concatenate([conv_state.astype(u.dtype), u], axis=1)
    v = w_conv[0] * u_ext[:, 0:L]
    for j in range(1, CONV_WIDTH):
        v = v + w_conv[j] * u_ext[:, j:j + L]
    y = jnp.einsum('bld,de->ble', b_gate * v, w_out)
    return y, u_ext[:, -(CONV_WIDTH - 1):]


def t5_bucket(n):
    max_exact = NUM_BUCKETS // 2
    nf = jnp.maximum(n, 1).astype(jnp.float32)
    large = max_exact + (jnp.log(nf / max_exact) / math.log(MAX_DISTANCE / max_exact)
                         * (NUM_BUCKETS - max_exact)).astype(jnp.int32)
    large = jnp.minimum(large, NUM_BUCKETS - 1)
    return jnp.where(n < max_exact, n, large)


def rel_bias_logits(table, q_pos, k_pos):
    n = jnp.maximum(q_pos[:, None] - k_pos[None, :], 0)
    bucket = t5_bucket(n)
    return jnp.transpose(table[bucket], (2, 0, 1)).astype(jnp.float32)


def lambda_value(lq1, lk1, lq2, lk2, lam_init):
    e1 = jnp.exp(jnp.sum(lq1.astype(jnp.float32) * lk1.astype(jnp.float32)))
    e2 = jnp.exp(jnp.sum(lq2.astype(jnp.float32) * lk2.astype(jnp.float32)))
    return e1 - e2 + lam_init


def shared_kv(x, w_kv):
    B, L, _ = x.shape
    kv = jnp.einsum('bld,de->ble', x, w_kv)
    k = kv[..., :QK_WIDTH].reshape(B, L, N_HEADS, 2, HEAD_DIM)
    v = kv[..., QK_WIDTH:].reshape(B, L, N_HEADS, V_DIM)
    return k, v


def diff_attn_core(q, k, v, bias, mask, lam):
    s = jnp.einsum('bqhcd,bkhcd->bhcqk', q, k, preferred_element_type=jnp.float32) * ATTN_SCALE
    s = s + bias[None, :, None]
    s = jnp.where(mask, s, -jnp.inf)
    p = jax.nn.softmax(s, axis=-1)
    a = p[:, :, 0] - lam * p[:, :, 1]
    return jnp.einsum('bhqk,bkhe->bqhe', a, v.astype(jnp.float32))


def diff_attn_output(o, lam_init, subln_g, w_o, dtype):
    B, L = o.shape[0], o.shape[1]
    o = o * lax.rsqrt(jnp.mean(o * o, axis=-1, keepdims=True) + RMS_EPS)
    o = o * subln_g.astype(jnp.float32) * (1.0 - lam_init)
    o = o.reshape(B, L, N_HEADS * V_DIM).astype(dtype)
    return jnp.einsum('ble,ed->bld', o, w_o)


def diff_attention_prompt(x, k, v, w_q, rel_bias, lam, lam_init, subln_g, w_o):
    B, L, _ = x.shape
    q = jnp.einsum('bld,de->ble', x, w_q).reshape(B, L, N_HEADS, 2, HEAD_DIM)
    nb = L // Q_BLOCK
    q_blocks = q.reshape(B, nb, Q_BLOCK, N_HEADS, 2, HEAD_DIM).transpose(1, 0, 2, 3, 4, 5)
    k_pos = jnp.arange(L)

    def block(args):
        qb, start = args
        q_pos = start + jnp.arange(Q_BLOCK)
        bias = rel_bias_logits(rel_bias, q_pos, k_pos)
        mask = k_pos[None, :] <= q_pos[:, None]
        return diff_attn_core(qb, k, v, bias, mask, lam)

    o = lax.map(block, (q_blocks, jnp.arange(nb) * Q_BLOCK))
    o = o.transpose(1, 0, 2, 3, 4).reshape(B, L, N_HEADS, V_DIM)
    return diff_attn_output(o, lam_init, subln_g, w_o, x.dtype)


def diff_attention_sample(x, k_all, v_all, past_len, w_q, rel_bias, lam, lam_init, subln_g, w_o):
    B, L, _ = x.shape
    q = jnp.einsum('bld,de->ble', x, w_q).reshape(B, L, N_HEADS, 2, HEAD_DIM)
    q_pos = past_len + jnp.arange(L)
    k_pos = jnp.arange(past_len + L)
    bias = rel_bias_logits(rel_bias, q_pos, k_pos)
    mask = k_pos[None, :] <= q_pos[:, None]
    o = diff_attn_core(q, k_all, v_all, bias, mask, lam)
    return diff_attn_output(o, lam_init, subln_g, w_o, x.dtype)


def moe(x, w_r, b_r, w_gu, b_gu, w_dn, b_dn):
    logits = (jnp.einsum('td,de->te', x, w_r) + b_r).astype(jnp.float32)
    top_v, top_i = lax.top_k(logits, TOP_K)
    wts = jax.nn.softmax(top_v, axis=-1)
    gates = jnp.sum(jax.nn.one_hot(top_i, N_EXPERTS, dtype=jnp.float32) * wts[..., None], axis=1)
    gates = gates.astype(x.dtype)

    def body(e, acc):
        gu = jnp.einsum('td,df->tf', x, w_gu[e]) + b_gu[e]
        g, u = gu[:, :D_FF], gu[:, D_FF:]
        g = jnp.minimum(g, SWIGLU_LIMIT)
        u = jnp.clip(u, -SWIGLU_LIMIT, SWIGLU_LIMIT)
        h = (u + 1.0) * (g * jax.nn.sigmoid(SWIGLU_ALPHA * g))
        out = jnp.einsum('tf,fd->td', h, w_dn[e]) + b_dn[e]
        return acc + gates[:, e][:, None] * out

    return lax.fori_loop(0, N_EXPERTS, body, jnp.zeros_like(x))


def setup_inputs(seed: int = 0) -> dict:
    key = jax.random.key(seed)
    ks = jax.random.split(key, 32)
    f32 = jnp.float32

    def init(k, shape, scale):
        a = math.sqrt(3.0) * scale
        return jax.random.uniform(k, shape, f32, -a, a)

    n_pages = PAST_LEN // PAGE_SIZE
    n_used = DEC_BATCH * n_pages
    n_pool = n_used + (n_used + 3) // 4
    page_table = jax.random.permutation(ks[0], n_pool)[:n_used].reshape(DEC_BATCH, n_pages).astype(jnp.int32)

    sd = D_MODEL ** -0.5
    w_kv = jnp.concatenate([init(ks[9], (D_MODEL, QK_WIDTH), sd),
                            init(ks[10], (D_MODEL, N_HEADS * V_DIM), sd * BETA)], axis=1)
    return {
        "x_prompt": jax.random.normal(ks[1], (BATCH, SEQ, D_MODEL), f32),
        "x_sample": jax.random.normal(ks[2], (DEC_BATCH, DEC_SEQ, D_MODEL), f32),
        "state_conv": jax.random.normal(ks[3], (N_A_LAYERS, DEC_BATCH, CONV_WIDTH - 1, D_MODEL), f32),
        "cache_k": init(ks[4], (n_pool, PAGE_SIZE, N_HEADS, 2, HEAD_DIM), 1.0),
        "cache_v": init(ks[5], (n_pool, PAGE_SIZE, N_HEADS, V_DIM), 1.0),
        "page_table": page_table,
        "conv_w_in": init(ks[6], (N_A_LAYERS, D_MODEL, 3 * D_MODEL), sd),
        "conv_w": init(ks[7], (N_A_LAYERS, CONV_WIDTH, D_MODEL), CONV_WIDTH ** -0.5),
        "conv_w_out": init(ks[8], (N_A_LAYERS, D_MODEL, D_MODEL), sd * BETA),
        "attn_w_q": init(ks[11], (N_B_LAYERS, D_MODEL, QK_WIDTH), sd),
        "attn_w_kv": w_kv,
        "lambda_q1": 0.1 * jax.random.normal(ks[12], (N_B_LAYERS, HEAD_DIM), f32),
        "lambda_k1": 0.1 * jax.random.normal(ks[13], (N_B_LAYERS, HEAD_DIM), f32),
        "lambda_q2": 0.1 * jax.random.normal(ks[14], (N_B_LAYERS, HEAD_DIM), f32),
        "lambda_k2": 0.1 * jax.random.normal(ks[15], (N_B_LAYERS, HEAD_DIM), f32),
        "subln_g": 1.0 + 0.02 * jax.random.normal(ks[16], (N_B_LAYERS, V_DIM), f32),
        "attn_w_o": init(ks[17], (N_B_LAYERS, N_HEADS * V_DIM, D_MODEL), sd * BETA),
        "rel_bias": 0.2 * jax.random.normal(ks[18], (NUM_BUCKETS, N_HEADS), f32),
        "ln_mix_g": 1.0 + 0.02 * jax.random.normal(ks[19], (DEPTH, D_MODEL), f32),
        "ln_mix_b": 0.02 * jax.random.normal(ks[20], (DEPTH, D_MODEL), f32),
        "ln_ffn_g": 1.0 + 0.02 * jax.random.normal(ks[21], (DEPTH, D_MODEL), f32),
        "ln_ffn_b": 0.02 * jax.random.normal(ks[22], (DEPTH, D_MODEL), f32),
        "router_w": init(ks[23], (DEPTH, D_MODEL, N_EXPERTS), sd),
        "router_b": 0.01 * jax.random.normal(ks[24], (DEPTH, N_EXPERTS), f32),
        "expert_w_gu": init(ks[25], (DEPTH, N_EXPERTS, D_MODEL, 2 * D_FF), sd),
        "expert_b_gu": 0.02 * jax.random.normal(ks[26], (DEPTH, N_EXPERTS, 2 * D_FF), f32),
        "expert_w_dn": init(ks[27], (DEPTH, N_EXPERTS, D_FF, D_MODEL), D_FF ** -0.5 * BETA),
        "expert_b_dn": 0.02 * jax.random.normal(ks[28], (DEPTH, N_EXPERTS, D_MODEL), f32),
    }


def reference(x_prompt, x_sample, state_conv, cache_k, cache_v, page_table,
              conv_w_in, conv_w, conv_w_out, attn_w_q, attn_w_kv,
              lambda_q1, lambda_k1, lambda_q2, lambda_k2, subln_g, attn_w_o, rel_bias,
              ln_mix_g, ln_mix_b, ln_ffn_g, ln_ffn_b, router_w, router_b,
              expert_w_gu, expert_b_gu, expert_w_dn, expert_b_dn):
    n_pages = page_table.shape[1]
    past_len = n_pages * cache_k.shape[1]
    xp, xs = x_prompt, x_sample
    bp, lp, _ = xp.shape
    bs, ls, _ = xs.shape
    conv_states_p, conv_states_s = [], []

    for l in range(DEPTH):
        if l < N_A_LAYERS:
            zero_state = jnp.zeros((bp, CONV_WIDTH - 1, D_MODEL), xp.dtype)
            mp, st_p = short_conv_mixer(xp, zero_state, conv_w_in[l], conv_w[l], conv_w_out[l])
            ms, st_s = short_conv_mixer(xs, state_conv[l], conv_w_in[l], conv_w[l], conv_w_out[l])
            conv_states_p.append(st_p)
            conv_states_s.append(st_s)
        else:
            if l == N_A_LAYERS:
                k_p, v_p = shared_kv(xp, attn_w_kv)
                k_s, v_s = shared_kv(xs, attn_w_kv)
                past_k = cache_k[page_table].reshape(bs, past_len, N_HEADS, 2, HEAD_DIM)
                past_v = cache_v[page_table].reshape(bs, past_len, N_HEADS, V_DIM)
                k_all = jnp.concatenate([past_k.astype(k_s.dtype), k_s], axis=1)
                v_all = jnp.concatenate([past_v.astype(v_s.dtype), v_s], axis=1)
            j = l - N_A_LAYERS
            lam_init = 0.8 - 0.6 * math.exp(-0.3 * l)
            lam = lambda_value(lambda_q1[j], lambda_k1[j], lambda_q2[j], lambda_k2[j], lam_init)
            mp = diff_attention_prompt(xp, k_p, v_p, attn_w_q[j], rel_bias, lam, lam_init,
                                       subln_g[j], attn_w_o[j])
            ms = diff_attention_sample(xs, k_all, v_all, past_len, attn_w_q[j], rel_bias, lam,
                                       lam_init, subln_g[j], attn_w_o[j])
        xp = layer_norm(ALPHA * xp + mp, ln_mix_g[l], ln_mix_b[l])
        xs = layer_norm(ALPHA * xs + ms, ln_mix_g[l], ln_mix_b[l])

        tokens = jnp.concatenate([xp.reshape(bp * lp, D_MODEL), xs.reshape(bs * ls, D_MODEL)], axis=0)
        f = moe(tokens, router_w[l], router_b[l], expert_w_gu[l], expert_b_gu[l],
                expert_w_dn[l], expert_b_dn[l])
        xp = layer_norm(ALPHA * xp + f[:bp * lp].reshape(xp.shape), ln_ffn_g[l], ln_ffn_b[l])
        xs = layer_norm(ALPHA * xs + f[bp * lp:].reshape(xs.shape), ln_ffn_g[l], ln_ffn_b[l])

    return (xp, xs, jnp.stack(conv_states_p), k_p, v_p, jnp.stack(conv_states_s), k_s, v_s)
```

```python
import functools
import math

import jax
import jax.numpy as jnp
from jax import lax
from jax.experimental import pallas as pl
from jax.experimental.pallas import tpu as pltpu

F32 = jnp.float32
BF16 = jnp.bfloat16
I32 = jnp.int32

DEPTH = 4
N_CONV_LAYERS = 2
CONV_WIDTH = 3
N_HEADS = 16
HEAD_DIM = 64
V_DIM = 128
NUM_BUCKETS = 32
MAX_DISTANCE = 128
N_EXPERTS = 32
TOP_K = 4
SWIGLU_LIMIT = 7.0
SWIGLU_ALPHA = 1.702
LN_EPS = 1e-5
RMS_EPS = 1e-5
ALPHA = (2 * DEPTH) ** 0.25
ATTN_SCALE = HEAD_DIM ** -0.5

ROW_TILE = 256
EXPERT_TILE = 256
FF_TILE = 512
CONV_COL_TILE = 512
ATTN_Q_TILE = 256
ATTN_K_TILE = 256
DEC_PAGES_PER_STEP = 4
VMEM_LIMIT = 56 * 1024 * 1024
LANES = 128


def _cparams(*sem):
    return pltpu.CompilerParams(dimension_semantics=sem, vmem_limit_bytes=VMEM_LIMIT)


def _dot(a, b):
    return jnp.dot(a, b, preferred_element_type=F32)


def _dot_nt(a, b):
    return lax.dot_general(a, b, (((1,), (1,)), ((), ())), preferred_element_type=F32)


def _layer_norm(r, g, b):
    mu = jnp.mean(r, axis=-1, keepdims=True)
    xc = r - mu
    var = jnp.mean(xc * xc, axis=-1, keepdims=True)
    return xc * lax.rsqrt(var + LN_EPS) * g + b


def _pack_bf16_pairs(xn):
    half = xn.shape[1] // 2
    hi = lax.bitcast_convert_type(xn[:, :half].astype(BF16).astype(F32), I32)
    lo = lax.bitcast_convert_type(xn[:, half:].astype(BF16).astype(F32), I32)
    return (hi & jnp.int32(-65536)) | lax.shift_right_logical(lo, jnp.int32(16))


def _unpack_bf16_pairs(pk):
    hi = lax.bitcast_convert_type(pk & jnp.int32(-65536), F32).astype(BF16)
    lo = lax.bitcast_convert_type(lax.shift_left(pk, jnp.int32(16)), F32).astype(BF16)
    return hi, lo


def _conv_in_kernel(x_ref, wb_ref, wc_ref, wh_ref, stp_ref, sts_ref, cw_ref,
                    z_ref, sop_ref, sos_ref, carry_ref, *, n_prompt_tiles, tiles_per_seq, dec_batch, dec_seq):
    i = pl.program_id(1)
    x = x_ref[...]
    bg = _dot(x, wb_ref[...])
    u = _dot(x, wc_ref[...]) * _dot(x, wh_ref[...])
    tm, tn = u.shape
    row = lax.broadcasted_iota(I32, (tm, tn), 0)
    s1 = pltpu.roll(u, 1, axis=0)
    s2 = pltpu.roll(u, 2, axis=0)
    cw = cw_ref[...]

    def finish(s1f, s2f):
        v = cw[0:1, :] * s2f + cw[1:2, :] * s1f + cw[2:3, :] * u
        z_ref[...] = (bg * v).astype(z_ref.dtype)

    @pl.when(i < n_prompt_tiles)
    def _():
        seq_start = (i % tiles_per_seq) == 0
        p1 = jnp.where(seq_start, stp_ref[0, 1:2, :], carry_ref[1:2, :])
        p2 = jnp.where(seq_start, stp_ref[0, 0:1, :], carry_ref[0:1, :])
        s1f = jnp.where(row == 0, p1, s1)
        s2f = jnp.where(row == 0, p2, jnp.where(row == 1, p1, s2))
        finish(s1f, s2f)
        tail = u[tm - 2:tm, :]
        carry_ref[0:2, :] = tail
        sop_ref[0] = tail

    @pl.when(i == n_prompt_tiles)
    def _():
        s1f, s2f = s1, s2
        for s in range(dec_batch):
            r0 = s * dec_seq
            p1 = sts_ref[s, 1:2, :]
            p2 = sts_ref[s, 0:1, :]
            s1f = jnp.where(row == r0, p1, s1f)
            s2f = jnp.where(row == r0, p2, jnp.where(row == r0 + 1, p1, s2f))
            sos_ref[s] = u[r0 + dec_seq - 2:r0 + dec_seq, :]
        finish(s1f, s2f)


def _conv_in(xb, w_in, st_prompt, st_sample, conv_w, *, n_prompt_tiles, tiles_per_seq, dec_batch, dec_seq):
    t_pad, d = xb.shape
    tm, tn = ROW_TILE, CONV_COL_TILE
    nb = d // tn
    n_seq_p = st_prompt.shape[0]
    grid = (nb, t_pad // tm)
    seq_of = lambda i: jnp.minimum(i // tiles_per_seq, n_seq_p - 1)
    kern = functools.partial(_conv_in_kernel, n_prompt_tiles=n_prompt_tiles, tiles_per_seq=tiles_per_seq,
                             dec_batch=dec_batch, dec_seq=dec_seq)
    return pl.pallas_call(
        kern,
        grid=grid,
        in_specs=[
            pl.BlockSpec((tm, d), lambda j, i: (i, 0)),
            pl.BlockSpec((d, tn), lambda j, i: (0, j)),
            pl.BlockSpec((d, tn), lambda j, i: (0, nb + j)),
            pl.BlockSpec((d, tn), lambda j, i: (0, 2 * nb + j)),
            pl.BlockSpec((1, CONV_WIDTH - 1, tn), lambda j, i: (seq_of(i), 0, j)),
            pl.BlockSpec((dec_batch, CONV_WIDTH - 1, tn), lambda j, i: (0, 0, j)),
            pl.BlockSpec((CONV_WIDTH, tn), lambda j, i: (0, j)),
        ],
        out_specs=[
            pl.BlockSpec((tm, tn), lambda j, i: (i, j)),
            pl.BlockSpec((1, CONV_WIDTH - 1, tn), lambda j, i: (seq_of(i), 0, j)),
            pl.BlockSpec((dec_batch, CONV_WIDTH - 1, tn), lambda j, i: (0, 0, j)),
        ],
        out_shape=[
            jax.ShapeDtypeStruct((t_pad, d), BF16),
            jax.ShapeDtypeStruct(st_prompt.shape, F32),
            jax.ShapeDtypeStruct(st_sample.shape, F32),
        ],
        scratch_shapes=[pltpu.VMEM((8, tn), F32)],
        compiler_params=_cparams("arbitrary", "arbitrary"),
    )(xb, w_in, w_in, w_in, st_prompt, st_sample, conv_w)


def _proj_router_kernel(z_ref, w_ref, x_ref, g_ref, b_ref, wrh_ref, wrl_ref, br_ref,
                        xo_ref, xpk_ref, meta_ref, wts_ref, cnt_ref, carry_ref, *, n_valid_rows):
    i = pl.program_id(0)
    tm = z_ref.shape[0]

    @pl.when(i == 0)
    def _():
        carry_ref[...] = jnp.zeros_like(carry_ref)

    y = _dot(z_ref[...], w_ref[...])
    xn = _layer_norm(ALPHA * x_ref[...] + y, g_ref[...], b_ref[...])
    xo_ref[...] = xn
    xpk_ref[...] = _pack_bf16_pairs(xn)

    xh = xn.astype(BF16)
    xl = (xn - xh.astype(F32)).astype(BF16)
    wrh = wrh_ref[...]
    logits = _dot(xh, wrh) + (_dot(xl, wrh) + _dot(xh, wrl_ref[...])) + br_ref[...]

    ne = logits.shape[1]
    lane = lax.broadcasted_iota(I32, (tm, ne), 1)
    lane_f = lane.astype(F32)
    work = logits
    ids, vals = [], []
    for _ in range(TOP_K):
        m = jnp.max(work, axis=1, keepdims=True)
        idx = jnp.min(jnp.where(work == m, lane_f, float(ne)), axis=1, keepdims=True).astype(I32)
        ids.append(idx)
        vals.append(m)
        work = jnp.where(lane == idx, -jnp.inf, work)
    exps = [jnp.exp(v - vals[0]) for v in vals]
    denom = exps[0] + exps[1] + exps[2] + exps[3]

    grow = i * tm + lax.broadcasted_iota(I32, (tm, 1), 0)
    valid = grow < n_valid_rows
    onehots = [jnp.logical_and(lane == idx, valid) for idx in ids]
    sel = onehots[0] | onehots[1] | onehots[2] | onehots[3]
    selb = jnp.where(sel, 1.0, 0.0).astype(BF16)
    tri = (lax.broadcasted_iota(I32, (tm, tm), 0) >= lax.broadcasted_iota(I32, (tm, tm), 1)).astype(BF16)
    incl = _dot(tri, selb)
    excl = incl - selb.astype(F32) + carry_ref[...]

    lane128 = lax.broadcasted_iota(I32, (tm, LANES), 1)
    meta = jnp.zeros((tm, LANES), I32)
    wts = jnp.zeros((tm, LANES), F32)
    for k in range(TOP_K):
        rank = jnp.sum(jnp.where(onehots[k], excl, 0.0), axis=1, keepdims=True).astype(I32)
        meta = jnp.where(lane128 == k, ids[k], meta)
        meta = jnp.where(lane128 == TOP_K + k, rank, meta)
        wts = jnp.where(lane128 == k, jnp.where(valid, exps[k] / denom, 0.0), wts)
    meta_ref[...] = meta
    wts_ref[...] = wts
    new_cnt = carry_ref[...] + incl[tm - 1:tm, :]
    carry_ref[...] = new_cnt
    cnt_ref[...] = new_cnt


def _proj_router(z, w, x, g, b, wr_hi, wr_lo, br, *, n_valid_rows):
    t_pad, d = x.shape
    kdim = z.shape[1]
    tm = ROW_TILE
    ne = wr_hi.shape[1]
    const = lambda i: (0, 0)
    row = lambda i: (i, 0)
    return pl.pallas_call(
        functools.partial(_proj_router_kernel, n_valid_rows=n_valid_rows),
        grid=(t_pad // tm,),
        in_specs=[
            pl.BlockSpec((tm, kdim), row),
            pl.BlockSpec((kdim, d), const, pipeline_mode=pl.Buffered(1)),
            pl.BlockSpec((tm, d), row),
            pl.BlockSpec((1, d), const),
            pl.BlockSpec((1, d), const),
            pl.BlockSpec((d, ne), const),
            pl.BlockSpec((d, ne), const),
            pl.BlockSpec((1, ne), const),
        ],
        out_specs=[
            pl.BlockSpec((tm, d), row),
            pl.BlockSpec((tm, d // 2), row),
            pl.BlockSpec((tm, LANES), row),
            pl.BlockSpec((tm, LANES), row),
            pl.BlockSpec((1, ne), const),
        ],
        out_shape=[
            jax.ShapeDtypeStruct((t_pad, d), F32),
            jax.ShapeDtypeStruct((t_pad, d // 2), I32),
            jax.ShapeDtypeStruct((t_pad, LANES), I32),
            jax.ShapeDtypeStruct((t_pad, LANES), F32),
            jax.ShapeDtypeStruct((1, ne), F32),
        ],
        scratch_shapes=[pltpu.VMEM((1, ne), F32)],
        compiler_params=_cparams("arbitrary"),
    )(z, w, x, g, b, wr_hi, wr_lo, br)


def _dispatch_kernel(pos_ref, xpk_ref, xs_in_ref, xs_ref, sem, *, n_valid_rows):
    del xs_in_ref
    i = pl.program_id(0)
    tm = xpk_ref.shape[0]
    n_rows = jnp.clip(n_valid_rows - i * tm, 0, tm)

    def row_copy(r, k):
        p = pos_ref[0, 0, r * TOP_K + k]
        return pltpu.make_async_copy(xpk_ref.at[pl.ds(r, 1), :], xs_ref.at[pl.ds(p, 1), :], sem)

    def start(r, c):
        for k in range(TOP_K):
            row_copy(r, k).start()
        return c

    def wait(r, c):
        for k in range(TOP_K):
            row_copy(r, k).wait()
        return c

    lax.fori_loop(0, n_rows, start, 0)
    lax.fori_loop(0, n_rows, wait, 0)


def _dispatch(pos_tiles, xpk, xs_init, *, n_valid_rows):
    t_pad, dh = xpk.shape
    tm = ROW_TILE
    return pl.pallas_call(
        functools.partial(_dispatch_kernel, n_valid_rows=n_valid_rows),
        grid=(t_pad // tm,),
        in_specs=[
            pl.BlockSpec((1, 1, tm * TOP_K), lambda i: (i, 0, 0), memory_space=pltpu.SMEM),
            pl.BlockSpec((tm, dh), lambda i: (i, 0)),
            pl.BlockSpec(memory_space=pl.ANY),
        ],
        out_specs=pl.BlockSpec(memory_space=pl.ANY),
        out_shape=jax.ShapeDtypeStruct(xs_init.shape, xs_init.dtype),
        scratch_shapes=[pltpu.SemaphoreType.DMA(())],
        input_output_aliases={2: 0},
        compiler_params=_cparams("arbitrary"),
    )(pos_tiles, xpk, xs_init)


def _expert_gu_kernel(te_ref, nv_ref, xs_ref, wg_ref, wu_ref, bg_ref, bu_ref, h_ref):
    del te_ref
    i = pl.program_id(1)

    @pl.when(i < nv_ref[0])
    def _():
        xhi, xlo = _unpack_bf16_pairs(xs_ref[...])
        half = xhi.shape[1]
        wg = wg_ref[...].astype(BF16)
        wu = wu_ref[...].astype(BF16)
        g = _dot(xhi, wg[:half]) + _dot(xlo, wg[half:]) + bg_ref[...]
        u = _dot(xhi, wu[:half]) + _dot(xlo, wu[half:]) + bu_ref[...]
        g = jnp.minimum(g, SWIGLU_LIMIT)
        u = jnp.clip(u, -SWIGLU_LIMIT, SWIGLU_LIMIT)
        h_ref[...] = ((u + 1.0) * (g * jax.nn.sigmoid(SWIGLU_ALPHA * g))).astype(h_ref.dtype)

    @pl.when(i >= nv_ref[0])
    def _():
        h_ref[...] = jnp.zeros_like(h_ref)


def _expert_gu(tile_e, n_valid, xs, w_gu, b_gu, layer):
    p_rows, dh = xs.shape
    d = 2 * dh
    dff = w_gu.shape[3] // 2
    tm, tn = EXPERT_TILE, FF_TILE
    nj = dff // tn
    tile = lambda j, i, te, nv: jnp.minimum(i, nv[0] - 1)
    return pl.pallas_call(
        _expert_gu_kernel,
        grid_spec=pltpu.PrefetchScalarGridSpec(
            num_scalar_prefetch=2,
            grid=(nj, p_rows // tm),
            in_specs=[
                pl.BlockSpec((tm, dh), lambda j, i, te, nv: (tile(j, i, te, nv), 0)),
                pl.BlockSpec((None, None, d, tn), lambda j, i, te, nv: (layer, te[i], 0, j)),
                pl.BlockSpec((None, None, d, tn), lambda j, i, te, nv: (layer, te[i], 0, nj + j)),
                pl.BlockSpec((None, None, 1, tn), lambda j, i, te, nv: (layer, te[i], 0, j)),
                pl.BlockSpec((None, None, 1, tn), lambda j, i, te, nv: (layer, te[i], 0, nj + j)),
            ],
            out_specs=pl.BlockSpec((tm, tn), lambda j, i, te, nv: (i, j)),
        ),
        out_shape=jax.ShapeDtypeStruct((p_rows, dff), BF16),
        compiler_params=_cparams("arbitrary", "arbitrary"),
    )(tile_e, n_valid, xs, w_gu, w_gu, b_gu, b_gu)


def _expert_dn_kernel(te_ref, nv_ref, h_ref, w_ref, b_ref, y_ref):
    del te_ref
    i = pl.program_id(1)

    @pl.when(i < nv_ref[0])
    def _():
        y_ref[...] = _dot(h_ref[...], w_ref[...].astype(BF16)) + b_ref[...]

    @pl.when(i >= nv_ref[0])
    def _():
        y_ref[...] = jnp.zeros_like(y_ref)


def _expert_dn(tile_e, n_valid, h, w_dn, b_dn, layer):
    p_rows, dff = h.shape
    d = w_dn.shape[3]
    tm, tn = EXPERT_TILE, FF_TILE
    nj = d // tn
    tile = lambda j, i, te, nv: jnp.minimum(i, nv[0] - 1)
    return pl.pallas_call(
        _expert_dn_kernel,
        grid_spec=pltpu.PrefetchScalarGridSpec(
            num_scalar_prefetch=2,
            grid=(nj, p_rows // tm),
            in_specs=[
                pl.BlockSpec((tm, dff), lambda j, i, te, nv: (tile(j, i, te, nv), 0)),
                pl.BlockSpec((None, None, dff, tn), lambda j, i, te, nv: (layer, te[i], 0, j)),
                pl.BlockSpec((None, None, 1, tn), lambda j, i, te, nv: (layer, te[i], 0, j)),
            ],
            out_specs=pl.BlockSpec((tm, tn), lambda j, i, te, nv: (i, j)),
        ),
        out_shape=jax.ShapeDtypeStruct((p_rows, d), F32),
        compiler_params=_cparams("arbitrary", "arbitrary"),
    )(tile_e, n_valid, h, w_dn, b_dn)


def _combine_kernel(pos_ref, wts_ref, x_ref, g_ref, b_ref, ys_ref, xo_ref, xb_ref, buf_ref, sem):
    tm = x_ref.shape[0]

    def row_copy(r, k):
        p = pos_ref[0, 0, r * TOP_K + k]
        return pltpu.make_async_copy(ys_ref.at[pl.ds(p, 1), :], buf_ref.at[k, pl.ds(r, 1), :], sem)

    def start(r, c):
        for k in range(TOP_K):
            row_copy(r, k).start()
        return c

    def wait(r, c):
        for k in range(TOP_K):
            row_copy(r, k).wait()
        return c

    lax.fori_loop(0, tm, start, 0)
    lax.fori_loop(0, tm, wait, 0)

    wts = wts_ref[...]
    f = wts[:, 0:1] * buf_ref[0]
    for k in range(1, TOP_K):
        f = f + wts[:, k:k + 1] * buf_ref[k]
    xn = _layer_norm(ALPHA * x_ref[...] + f, g_ref[...], b_ref[...])
    xo_ref[...] = xn
    xb_ref[...] = xn.astype(BF16)


def _combine(pos_tiles, wts, x, g, b, ys):
    t_pad, d = x.shape
    tm = ROW_TILE
    return pl.pallas_call(
        _combine_kernel,
        grid=(t_pad // tm,),
        in_specs=[
            pl.BlockSpec((1, 1, tm * TOP_K), lambda i: (i, 0, 0), memory_space=pltpu.SMEM),
            pl.BlockSpec((tm, LANES), lambda i: (i, 0)),
            pl.BlockSpec((tm, d), lambda i: (i, 0)),
            pl.BlockSpec((1, d), lambda i: (0, 0)),
            pl.BlockSpec((1, d), lambda i: (0, 0)),
            pl.BlockSpec(memory_space=pl.ANY),
        ],
        out_specs=[
            pl.BlockSpec((tm, d), lambda i: (i, 0)),
            pl.BlockSpec((tm, d), lambda i: (i, 0)),
        ],
        out_shape=[
            jax.ShapeDtypeStruct((t_pad, d), F32),
            jax.ShapeDtypeStruct((t_pad, d), BF16),
        ],
        scratch_shapes=[pltpu.VMEM((TOP_K, tm, d), F32), pltpu.SemaphoreType.DMA(())],
        compiler_params=_cparams("arbitrary"),
    )(pos_tiles, wts, x, g, b, ys)


def _moe(xo, xpk, meta, wts, counts, ln_g, ln_b, w_gu, b_gu, w_dn, b_dn, layer, *, n_valid_rows):
    t_pad, d = xo.shape
    tm = ROW_TILE
    te = EXPERT_TILE
    n_tiles = -(-(n_valid_rows * TOP_K + N_EXPERTS * (te - 1)) // te)
    cnt = counts[0].astype(I32)
    padded = ((cnt + te - 1) // te) * te
    ends = jnp.cumsum(padded)
    starts = ends - padded
    ids = meta[:, :TOP_K]
    rank = meta[:, TOP_K:2 * TOP_K]
    valid = (jnp.arange(t_pad, dtype=I32) < n_valid_rows)[:, None]
    pos = jnp.where(valid, jnp.take(starts, ids) + rank, 0).astype(I32)
    pos_tiles = pos.reshape(t_pad // tm, 1, tm * TOP_K)
    n_valid_tiles = (ends[-1] // te).astype(I32)
    tile_start = jnp.arange(n_tiles, dtype=I32) * te
    tile_e = jnp.minimum(jnp.searchsorted(ends, tile_start, side="right"), N_EXPERTS - 1).astype(I32)
    tile_e = jnp.where(jnp.arange(n_tiles) < n_valid_tiles, tile_e, tile_e[jnp.maximum(n_valid_tiles - 1, 0)])
    nv = n_valid_tiles.reshape(1)

    xs = _dispatch(pos_tiles, xpk, jnp.zeros((n_tiles * te, d // 2), I32), n_valid_rows=n_valid_rows)
    h = _expert_gu(tile_e, nv, xs, w_gu, b_gu, layer)
    ys = _expert_dn(tile_e, nv, h, w_dn, b_dn, layer)
    return _combine(pos_tiles, wts, xo, ln_g, ln_b, ys)


def _kv_proj_kernel(x_ref, w_ref, kf_ref, vf_ref, kb_ref, vb_ref):
    kv = _dot(x_ref[...], w_ref[...])
    half = kv.shape[1] // 2
    k = kv[:, :half]
    v = kv[:, half:]
    kf_ref[...] = k
    vf_ref[...] = v
    kb_ref[...] = k.astype(BF16)
    vb_ref[...] = v.astype(BF16)


def _kv_proj(xb, w_kv):
    t_pad, d = xb.shape
    n = w_kv.shape[1] // 2
    tm = ROW_TILE
    row = lambda i: (i, 0)
    return pl.pallas_call(
        _kv_proj_kernel,
        grid=(t_pad // tm,),
        in_specs=[pl.BlockSpec((tm, d), row),
                  pl.BlockSpec((d, 2 * n), lambda i: (0, 0), pipeline_mode=pl.Buffered(1))],
        out_specs=[pl.BlockSpec((tm, n), row)] * 4,
        out_shape=[jax.ShapeDtypeStruct((t_pad, n), F32), jax.ShapeDtypeStruct((t_pad, n), F32),
                   jax.ShapeDtypeStruct((t_pad, n), BF16), jax.ShapeDtypeStruct((t_pad, n), BF16)],
        compiler_params=_cparams("arbitrary"),
    )(xb, w_kv)


def _q_proj_kernel(x_ref, w_ref, q_ref):
    q_ref[...] = (_dot(x_ref[...], w_ref[...]) * ATTN_SCALE).astype(BF16)


def _q_proj(xb, w_q):
    t_pad, d = xb.shape
    n = w_q.shape[1]
    tm = ROW_TILE
    return pl.pallas_call(
        _q_proj_kernel,
        grid=(t_pad // tm,),
        in_specs=[pl.BlockSpec((tm, d), lambda i: (i, 0)),
                  pl.BlockSpec((d, n), lambda i: (0, 0), pipeline_mode=pl.Buffered(1))],
        out_specs=pl.BlockSpec((tm, n), lambda i: (i, 0)),
        out_shape=jax.ShapeDtypeStruct((t_pad, n), BF16),
        compiler_params=_cparams("arbitrary"),
    )(xb, w_q)


def _lambda_value(lam_ref, lam_init):
    e1 = jnp.exp(jnp.sum(lam_ref[0:1, :] * lam_ref[1:2, :], axis=1, keepdims=True))
    e2 = jnp.exp(jnp.sum(lam_ref[2:3, :] * lam_ref[3:4, :], axis=1, keepdims=True))
    return e1 - e2 + lam_init


def _bias_from_buckets(bucket, tab_ref, head):
    far = tab_ref[NUM_BUCKETS - 1, head]
    bias = jnp.where(bucket < 0, -jnp.inf, 0.0).astype(F32)
    for b in range(NUM_BUCKETS - 1):
        bias = jnp.where(bucket == b, tab_ref[b, head] - far, bias)
    return bias


def _diff_finalize(acc, l, lam, g, lam_init):
    n = acc.shape[0] // 2
    o = acc[:n] / l[:n] - lam * (acc[n:] / l[n:])
    o = o * lax.rsqrt(jnp.mean(o * o, axis=-1, keepdims=True) + RMS_EPS)
    return o * g * (1.0 - lam_init)


def _prompt_attn_kernel(tab_ref, q_ref, k_ref, v_ref, bkt_ref, lam_ref, g_ref, o_ref, bias_ref, *, lam_init):
    h = pl.program_id(1)
    qi = pl.program_id(2)
    tq = q_ref.shape[0]
    tk = bkt_ref.shape[2]

    @pl.when(qi == 0)
    def _():
        bias_ref[0] = _bias_from_buckets(bkt_ref[0], tab_ref, h)
        bias_ref[1] = _bias_from_buckets(bkt_ref[1], tab_ref, h)
        bias_ref[2] = jnp.zeros((tq, tk), F32)

    q = q_ref[...]
    lane = lax.broadcasted_iota(I32, q.shape, 1)
    zero = jnp.zeros_like(q)
    qs = jnp.concatenate([jnp.where(lane < HEAD_DIM, q, zero), jnp.where(lane >= HEAD_DIM, q, zero)], axis=0)

    def step(ki, carry):
        m, l, acc = carry
        start = pl.multiple_of(ki * tk, tk)
        kb = k_ref[pl.ds(start, tk), :]
        vb = v_ref[pl.ds(start, tk), :]
        bias = bias_ref[jnp.minimum(qi - ki, 2)]
        s = _dot_nt(qs, kb) + jnp.concatenate([bias, bias], axis=0)
        m_new = jnp.maximum(m, jnp.max(s, axis=1, keepdims=True))
        p = jnp.exp(s - m_new)
        corr = jnp.exp(m - m_new)
        l = corr * l + jnp.sum(p, axis=1, keepdims=True)
        acc = corr * acc + _dot(p.astype(BF16), vb)
        return m_new, l, acc

    init = (jnp.full((2 * tq, 1), -jnp.inf, F32), jnp.zeros((2 * tq, 1), F32), jnp.zeros((2 * tq, V_DIM), F32))
    _, l, acc = lax.fori_loop(0, qi + 1, step, init)
    lam = _lambda_value(lam_ref, lam_init)
    o_ref[...] = _diff_finalize(acc, l, lam, g_ref[...], lam_init).astype(o_ref.dtype)


def _prompt_attn(q, kb, vb, buckets, table, lam_vecs, g, *, batch, seq, lam_init):
    tq, tk = ATTN_Q_TILE, ATTN_K_TILE
    nq = seq // tq
    return pl.pallas_call(
        functools.partial(_prompt_attn_kernel, lam_init=lam_init),
        grid_spec=pltpu.PrefetchScalarGridSpec(
            num_scalar_prefetch=1,
            grid=(batch, N_HEADS, nq),
            in_specs=[
                pl.BlockSpec((tq, V_DIM), lambda b, h, i, tab: (b * nq + i, h)),
                pl.BlockSpec((seq, V_DIM), lambda b, h, i, tab: (b, h)),
                pl.BlockSpec((seq, V_DIM), lambda b, h, i, tab: (b, h)),
                pl.BlockSpec((2, tq, tk), lambda b, h, i, tab: (0, 0, 0)),
                pl.BlockSpec((4, HEAD_DIM), lambda b, h, i, tab: (0, 0)),
                pl.BlockSpec((1, V_DIM), lambda b, h, i, tab: (0, 0)),
            ],
            out_specs=pl.BlockSpec((tq, V_DIM), lambda b, h, i, tab: (b * nq + i, h)),
            scratch_shapes=[pltpu.VMEM((3, tq, tk), F32)],
        ),
        out_shape=jax.ShapeDtypeStruct((batch * seq, N_HEADS * V_DIM), BF16),
        compiler_params=_cparams("arbitrary", "arbitrary", "arbitrary"),
    )(table, q, kb, vb, buckets, lam_vecs, g)


def _sample_attn_kernel(pt_ref, tab_ref, q_ref, kn_ref, vn_ref, bkt_last_ref, bkt_new_ref, lam_ref, g_ref, *refs,
                        lam_init, n_pages_step):
    del pt_ref
    k_refs = refs[:n_pages_step]
    v_refs = refs[n_pages_step:2 * n_pages_step]
    o_ref, m_ref, l_ref, acc_ref, bias_ref = refs[2 * n_pages_step:]
    c = pl.program_id(1)
    nc = pl.num_programs(1)
    lq = q_ref.shape[1]
    page = k_refs[0].shape[1]

    @pl.when(c == 0)
    def _():
        m_ref[...] = jnp.full(m_ref.shape, -jnp.inf, F32)
        l_ref[...] = jnp.zeros(l_ref.shape, F32)
        acc_ref[...] = jnp.zeros(acc_ref.shape, F32)
        bias_ref[...] = jnp.zeros(bias_ref.shape, F32)

    @pl.when(c == nc - 1)
    def _():
        for h in range(N_HEADS):
            bias_ref[h] = _bias_from_buckets(bkt_last_ref[...], tab_ref, h)

    def stacked_q(h):
        q = q_ref[0, :, h * V_DIM:(h + 1) * V_DIM]
        lane = lax.broadcasted_iota(I32, q.shape, 1)
        zero = jnp.zeros_like(q)
        qs = jnp.concatenate([jnp.where(lane < HEAD_DIM, q, zero), jnp.where(lane >= HEAD_DIM, q, zero)], axis=0)
        return qs.astype(BF16)

    def update(h, s, v_list):
        m_old = m_ref[h]
        m_new = jnp.maximum(m_old, jnp.max(s, axis=1, keepdims=True))
        p = jnp.exp(s - m_new)
        corr = jnp.exp(m_old - m_new)
        l_ref[h] = corr * l_ref[h] + jnp.sum(p, axis=1, keepdims=True)
        pv = _dot(p[:, 0:page].astype(BF16), v_list[0])
        for j in range(1, len(v_list)):
            pv = pv + _dot(p[:, j * page:(j + 1) * page].astype(BF16), v_list[j])
        acc_ref[h] = corr * acc_ref[h] + pv
        m_ref[h] = m_new

    for h in range(N_HEADS):
        qs = stacked_q(h)
        cols = slice(h * V_DIM, (h + 1) * V_DIM)
        s = jnp.concatenate([_dot_nt(qs, kr[0, :, cols].astype(BF16)) for kr in k_refs], axis=1)
        bias = bias_ref[h]
        s = s + jnp.concatenate([bias, bias], axis=0)
        update(h, s, [vr[0, :, cols].astype(BF16) for vr in v_refs])

    @pl.when(c == nc - 1)
    def _():
        lam = _lambda_value(lam_ref, lam_init)
        for h in range(N_HEADS):
            qs = stacked_q(h)
            cols = slice(h * V_DIM, (h + 1) * V_DIM)
            bias = _bias_from_buckets(bkt_new_ref[...], tab_ref, h)
            s = _dot_nt(qs, kn_ref[0, :, cols]) + jnp.concatenate([bias, bias], axis=0)
            update(h, s, [vn_ref[0, :, cols]])
            o = _diff_finalize(acc_ref[h], l_ref[h], lam, g_ref[...], lam_init)
            o_ref[0, :, cols] = o.astype(o_ref.dtype)


def _sample_attn(q_s, k_new, v_new, cache_k, cache_v, page_table, bkt_last, bkt_new, table, lam_vecs, g, *, lam_init):
    bsz, lq, dm = q_s.shape
    page = cache_k.shape[1]
    n_pages = page_table.shape[1]
    pg = DEC_PAGES_PER_STEP
    nc = n_pages // pg
    page_specs = [
        pl.BlockSpec((1, page, dm), functools.partial(lambda b, c, pt, tab, r: (pt[b, c * pg + r], 0, 0), r=r))
        for r in range(pg)
    ]
    per_b = lambda b, c, pt, tab: (b, 0, 0)
    const2 = lambda b, c, pt, tab: (0, 0)
    return pl.pallas_call(
        functools.partial(_sample_attn_kernel, lam_init=lam_init, n_pages_step=pg),
        grid_spec=pltpu.PrefetchScalarGridSpec(
            num_scalar_prefetch=2,
            grid=(bsz, nc),
            in_specs=[
                pl.BlockSpec((1, lq, dm), per_b),
                pl.BlockSpec((1, page, dm), per_b),
                pl.BlockSpec((1, page, dm), per_b),
                pl.BlockSpec((lq, pg * page), const2),
                pl.BlockSpec((lq, page), const2),
                pl.BlockSpec((4, HEAD_DIM), const2),
                pl.BlockSpec((1, V_DIM), const2),
            ] + page_specs + page_specs,
            out_specs=pl.BlockSpec((1, lq, dm), per_b),
            scratch_shapes=[
                pltpu.VMEM((N_HEADS, 2 * lq, 1), F32),
                pltpu.VMEM((N_HEADS, 2 * lq, 1), F32),
                pltpu.VMEM((N_HEADS, 2 * lq, V_DIM), F32),
                pltpu.VMEM((N_HEADS, lq, pg * page), F32),
            ],
        ),
        out_shape=jax.ShapeDtypeStruct((bsz, lq, dm), F32),
        compiler_params=_cparams("arbitrary", "arbitrary"),
    )(page_table, table, q_s, k_new, v_new, bkt_last, bkt_new, lam_vecs, g, *([cache_k] * pg), *([cache_v] * pg))


def _t5_bucket(n):
    max_exact = NUM_BUCKETS // 2
    nf = jnp.maximum(n, 1).astype(F32)
    large = max_exact + (jnp.log(nf / max_exact) / math.log(MAX_DISTANCE / max_exact)
                         * (NUM_BUCKETS - max_exact)).astype(I32)
    large = jnp.minimum(large, NUM_BUCKETS - 1)
    return jnp.where(n < 0, -1, jnp.where(n < max_exact, n, large)).astype(I32)


def kernel(x_prompt, x_sample, state_conv, cache_k, cache_v, page_table, conv_w_in, conv_w, conv_w_out, attn_w_q,
           attn_w_kv, lambda_q1, lambda_k1, lambda_q2, lambda_k2, subln_g, attn_w_o, rel_bias, ln_mix_g, ln_mix_b,
           ln_ffn_g, ln_ffn_b, router_w, router_b, expert_w_gu, expert_b_gu, expert_w_dn, expert_b_dn):
    bp, lp, d = x_prompt.shape
    bs, ls, _ = x_sample.shape
    n_pool, page = cache_k.shape[0], cache_k.shape[1]
    n_pages = page_table.shape[1]
    past_len = n_pages * page
    tp, ts = bp * lp, bs * ls
    t_valid = tp + ts
    tm = ROW_TILE
    assert lp % tm == 0 and ts <= tm and lp % ATTN_Q_TILE == 0 and ls >= CONV_WIDTH - 1
    assert n_pages % DEC_PAGES_PER_STEP == 0 and page >= ls
    n_prompt_tiles = tp // tm
    t_pad = tp + tm
    assert ATTN_K_TILE >= MAX_DISTANCE and DEC_PAGES_PER_STEP * page >= MAX_DISTANCE

    x = jnp.concatenate([x_prompt.reshape(tp, d), x_sample.reshape(ts, d), jnp.zeros((t_pad - t_valid, d), F32)], axis=0)
    xb = x.astype(BF16)

    tq, tk = ATTN_Q_TILE, ATTN_K_TILE
    rq = jnp.arange(tq, dtype=I32)[:, None]
    ck = jnp.arange(tk, dtype=I32)[None, :]
    bkt_prompt = jnp.stack([_t5_bucket(rq - ck), _t5_bucket(tk + rq - ck)])
    rs = jnp.arange(ls, dtype=I32)[:, None]
    chunk = DEC_PAGES_PER_STEP * page
    bkt_last = _t5_bucket(chunk + rs - jnp.arange(chunk, dtype=I32)[None, :])
    bkt_new = _t5_bucket(rs - jnp.arange(page, dtype=I32)[None, :])

    st_prompt = jnp.zeros((bp, CONV_WIDTH - 1, d), F32)
    conv_states_p, conv_states_s = [], []
    k_f = v_f = k_b = v_b = None
    cache_k2 = cache_k.reshape(n_pool, page, d)
    cache_v2 = cache_v.reshape(n_pool, page, d)

    for l in range(DEPTH):
        if l < N_CONV_LAYERS:
            z, st_p, st_s = _conv_in(xb, conv_w_in[l].astype(BF16), st_prompt, state_conv[l], conv_w[l],
                                     n_prompt_tiles=n_prompt_tiles, tiles_per_seq=lp // tm, dec_batch=bs, dec_seq=ls)
            conv_states_p.append(st_p)
            conv_states_s.append(st_s)
            w_out = conv_w_out[l].astype(BF16)
        else:
            j = l - N_CONV_LAYERS
            if j == 0:
                k_f, v_f, k_b, v_b = _kv_proj(xb, attn_w_kv.astype(BF16))
                k_new = jnp.zeros((bs, page, d), BF16).at[:, :ls].set(k_b[tp:t_valid].reshape(bs, ls, d))
                v_new = jnp.zeros((bs, page, d), BF16).at[:, :ls].set(v_b[tp:t_valid].reshape(bs, ls, d))
            lam_init = 0.8 - 0.6 * math.exp(-0.3 * l)
            lam_vecs = jnp.stack([lambda_q1[j], lambda_k1[j], lambda_q2[j], lambda_k2[j]])
            g = subln_g[j].reshape(1, V_DIM)
            q = _q_proj(xb, attn_w_q[j].astype(BF16))
            o_p = _prompt_attn(q, k_b, v_b, bkt_prompt, rel_bias, lam_vecs, g, batch=bp, seq=lp, lam_init=lam_init)
            o_s = _sample_attn(q[tp:t_valid].astype(F32).reshape(bs, ls, d), k_new, v_new, cache_k2, cache_v2,
                               page_table, bkt_last, bkt_new, rel_bias, lam_vecs, g, lam_init=lam_init)
            z = jnp.concatenate([o_p, o_s.astype(BF16).reshape(ts, d), jnp.zeros((t_pad - t_valid, d), BF16)], axis=0)
            w_out = attn_w_o[j].astype(BF16)

        wr = router_w[l]
        wr_hi = wr.astype(BF16)
        wr_lo = (wr - wr_hi.astype(F32)).astype(BF16)
        xo, xpk, meta, wts, counts = _proj_router(
            z, w_out, x, ln_mix_g[l].reshape(1, d), ln_mix_b[l].reshape(1, d), wr_hi, wr_lo,
            router_b[l].reshape(1, N_EXPERTS), n_valid_rows=t_valid)
        x, xb = _moe(xo, xpk, meta, wts, counts, ln_ffn_g[l].reshape(1, d), ln_ffn_b[l].reshape(1, d),
                     expert_w_gu, expert_b_gu.reshape(DEPTH, N_EXPERTS, 1, -1), expert_w_dn,
                     expert_b_dn.reshape(DEPTH, N_EXPERTS, 1, -1), l, n_valid_rows=t_valid)

    hk = (N_HEADS, 2, HEAD_DIM)
    hv = (N_HEADS, V_DIM)
    return (x[:tp].reshape(bp, lp, d), x[tp:t_valid].reshape(bs, ls, d), jnp.stack(conv_states_p),
            k_f[:tp].reshape(bp, lp, *hk), v_f[:tp].reshape(bp, lp, *hv), jnp.stack(conv_states_s),
            k_f[tp:t_valid].reshape(bs, ls, *hk), v_f[tp:t_valid].reshape(bs, ls, *hv))
```

```python
import functools
import math

import jax
import jax.numpy as jnp
from jax import lax
from jax.experimental import pallas as pl
from jax.experimental.pallas import tpu as pltpu

F32 = jnp.float32
BF16 = jnp.bfloat16
I32 = jnp.int32

DEPTH = 4
N_CONV_LAYERS = 2
CONV_WIDTH = 3
N_HEADS = 16
HEAD_DIM = 64
V_DIM = 128
NUM_BUCKETS = 32
MAX_DISTANCE = 128
N_EXPERTS = 32
TOP_K = 4
SWIGLU_LIMIT = 7.0
SWIGLU_ALPHA = 1.702
LN_EPS = 1e-5
RMS_EPS = 1e-5
ALPHA = (2 * DEPTH) ** 0.25
ATTN_SCALE = HEAD_DIM ** -0.5

ROW_TILE = 256
EXPERT_TILE = 256
FF_TILE = 1024
CONV_COL_TILE = 512
ATTN_Q_TILE = 256
ATTN_K_TILE = 256
DEC_PAGES_PER_STEP = 4
VMEM_LIMIT = 56 * 1024 * 1024
LANES = 128


def _cparams(*sem):
    return pltpu.CompilerParams(dimension_semantics=sem, vmem_limit_bytes=VMEM_LIMIT)


def _dot(a, b):
    return jnp.dot(a, b, preferred_element_type=F32)


def _dot_nt(a, b):
    return lax.dot_general(a, b, (((1,), (1,)), ((), ())), preferred_element_type=F32)


def _layer_norm(r, g, b):
    mu = jnp.mean(r, axis=-1, keepdims=True)
    xc = r - mu
    var = jnp.mean(xc * xc, axis=-1, keepdims=True)
    return xc * lax.rsqrt(var + LN_EPS) * g + b


def _pack_bf16_pairs(xn):
    half = xn.shape[1] // 2
    hi = lax.bitcast_convert_type(xn[:, :half].astype(BF16).astype(F32), I32)
    lo = lax.bitcast_convert_type(xn[:, half:].astype(BF16).astype(F32), I32)
    return (hi & jnp.int32(-65536)) | lax.shift_right_logical(lo, jnp.int32(16))


def _unpack_bf16_pairs(pk):
    hi = lax.bitcast_convert_type(pk & jnp.int32(-65536), F32).astype(BF16)
    lo = lax.bitcast_convert_type(lax.shift_left(pk, jnp.int32(16)), F32).astype(BF16)
    return hi, lo


def _conv_in_kernel(x_ref, wb_ref, wc_ref, wh_ref, stp_ref, sts_ref, cw_ref,
                    z_ref, sop_ref, sos_ref, carry_ref, *, n_prompt_tiles, tiles_per_seq, dec_batch, dec_seq):
    i = pl.program_id(1)
    x = x_ref[...]
    bg = _dot(x, wb_ref[...])
    u = _dot(x, wc_ref[...]) * _dot(x, wh_ref[...])
    tm, tn = u.shape
    row = lax.broadcasted_iota(I32, (tm, tn), 0)
    s1 = pltpu.roll(u, 1, axis=0)
    s2 = pltpu.roll(u, 2, axis=0)
    cw = cw_ref[...]

    def finish(s1f, s2f):
        v = cw[0:1, :] * s2f + cw[1:2, :] * s1f + cw[2:3, :] * u
        z_ref[...] = (bg * v).astype(z_ref.dtype)

    @pl.when(i < n_prompt_tiles)
    def _():
        seq_start = (i % tiles_per_seq) == 0
        p1 = jnp.where(seq_start, stp_ref[0, 1:2, :], carry_ref[1:2, :])
        p2 = jnp.where(seq_start, stp_ref[0, 0:1, :], carry_ref[0:1, :])
        s1f = jnp.where(row == 0, p1, s1)
        s2f = jnp.where(row == 0, p2, jnp.where(row == 1, p1, s2))
        finish(s1f, s2f)
        tail = u[tm - 2:tm, :]
        carry_ref[0:2, :] = tail
        sop_ref[0] = tail

    @pl.when(i == n_prompt_tiles)
    def _():
        s1f, s2f = s1, s2
        for s in range(dec_batch):
            r0 = s * dec_seq
            p1 = sts_ref[s, 1:2, :]
            p2 = sts_ref[s, 0:1, :]
            s1f = jnp.where(row == r0, p1, s1f)
            s2f = jnp.where(row == r0, p2, jnp.where(row == r0 + 1, p1, s2f))
            sos_ref[s] = u[r0 + dec_seq - 2:r0 + dec_seq, :]
        finish(s1f, s2f)


def _conv_in(xb, w_in, st_prompt, st_sample, conv_w, *, n_prompt_tiles, tiles_per_seq, dec_batch, dec_seq):
    t_pad, d = xb.shape
    tm, tn = ROW_TILE, CONV_COL_TILE
    nb = d // tn
    n_seq_p = st_prompt.shape[0]
    grid = (nb, t_pad // tm)
    seq_of = lambda i: jnp.minimum(i // tiles_per_seq, n_seq_p - 1)
    kern = functools.partial(_conv_in_kernel, n_prompt_tiles=n_prompt_tiles, tiles_per_seq=tiles_per_seq,
                             dec_batch=dec_batch, dec_seq=dec_seq)
    return pl.pallas_call(
        kern,
        grid=grid,
        in_specs=[
            pl.BlockSpec((tm, d), lambda j, i: (i, 0)),
            pl.BlockSpec((d, tn), lambda j, i: (0, j)),
            pl.BlockSpec((d, tn), lambda j, i: (0, nb + j)),
            pl.BlockSpec((d, tn), lambda j, i: (0, 2 * nb + j)),
            pl.BlockSpec((1, CONV_WIDTH - 1, tn), lambda j, i: (seq_of(i), 0, j)),
            pl.BlockSpec((dec_batch, CONV_WIDTH - 1, tn), lambda j, i: (0, 0, j)),
            pl.BlockSpec((CONV_WIDTH, tn), lambda j, i: (0, j)),
        ],
        out_specs=[
            pl.BlockSpec((tm, tn), lambda j, i: (i, j)),
            pl.BlockSpec((1, CONV_WIDTH - 1, tn), lambda j, i: (seq_of(i), 0, j)),
            pl.BlockSpec((dec_batch, CONV_WIDTH - 1, tn), lambda j, i: (0, 0, j)),
        ],
        out_shape=[
            jax.ShapeDtypeStruct((t_pad, d), BF16),
            jax.ShapeDtypeStruct(st_prompt.shape, F32),
            jax.ShapeDtypeStruct(st_sample.shape, F32),
        ],
        scratch_shapes=[pltpu.VMEM((8, tn), F32)],
        compiler_params=_cparams("arbitrary", "arbitrary"),
        name="conv_in",
    )(xb, w_in, w_in, w_in, st_prompt, st_sample, conv_w)


def _proj_router_kernel(z_ref, w_ref, x_ref, g_ref, b_ref, wrh_ref, wrl_ref, br_ref,
                        xo_ref, xpk_ref, meta_ref, wts_ref, cnt_ref, carry_ref, *, n_valid_rows):
    i = pl.program_id(0)
    tm = z_ref.shape[0]

    @pl.when(i == 0)
    def _():
        carry_ref[...] = jnp.zeros_like(carry_ref)

    y = _dot(z_ref[...], w_ref[...])
    xn = _layer_norm(ALPHA * x_ref[...] + y, g_ref[...], b_ref[...])
    xo_ref[...] = xn
    xpk_ref[...] = _pack_bf16_pairs(xn)

    xh = xn.astype(BF16)
    xl = (xn - xh.astype(F32)).astype(BF16)
    wrh = wrh_ref[...]
    logits = _dot(xh, wrh) + (_dot(xl, wrh) + _dot(xh, wrl_ref[...])) + br_ref[...]

    ne = logits.shape[1]
    lane = lax.broadcasted_iota(I32, (tm, ne), 1)
    lane_f = lane.astype(F32)
    work = logits
    ids, vals = [], []
    for _ in range(TOP_K):
        m = jnp.max(work, axis=1, keepdims=True)
        idx = jnp.min(jnp.where(work == m, lane_f, float(ne)), axis=1, keepdims=True).astype(I32)
        ids.append(idx)
        vals.append(m)
        work = jnp.where(lane == idx, -jnp.inf, work)
    exps = [jnp.exp(v - vals[0]) for v in vals]
    denom = exps[0] + exps[1] + exps[2] + exps[3]

    grow = i * tm + lax.broadcasted_iota(I32, (tm, 1), 0)
    valid = grow < n_valid_rows
    onehots = [jnp.logical_and(lane == idx, valid) for idx in ids]
    sel = onehots[0] | onehots[1] | onehots[2] | onehots[3]
    selb = jnp.where(sel, 1.0, 0.0).astype(BF16)
    tri = (lax.broadcasted_iota(I32, (tm, tm), 0) >= lax.broadcasted_iota(I32, (tm, tm), 1)).astype(BF16)
    incl = _dot(tri, selb)
    excl = incl - selb.astype(F32) + carry_ref[...]

    lane128 = lax.broadcasted_iota(I32, (tm, LANES), 1)
    meta = jnp.zeros((tm, LANES), I32)
    wts = jnp.zeros((tm, LANES), F32)
    for k in range(TOP_K):
        rank = jnp.sum(jnp.where(onehots[k], excl, 0.0), axis=1, keepdims=True).astype(I32)
        meta = jnp.where(lane128 == k, ids[k], meta)
        meta = jnp.where(lane128 == TOP_K + k, rank, meta)
        wts = jnp.where(lane128 == k, jnp.where(valid, exps[k] / denom, 0.0), wts)
    meta_ref[...] = meta
    wts_ref[...] = wts
    new_cnt = carry_ref[...] + incl[tm - 1:tm, :]
    carry_ref[...] = new_cnt
    cnt_ref[...] = new_cnt


def _proj_router(z, w, x, g, b, wr_hi, wr_lo, br, *, n_valid_rows):
    t_pad, d = x.shape
    kdim = z.shape[1]
    tm = ROW_TILE
    ne = wr_hi.shape[1]
    const = lambda i: (0, 0)
    row = lambda i: (i, 0)
    return pl.pallas_call(
        functools.partial(_proj_router_kernel, n_valid_rows=n_valid_rows),
        grid=(t_pad // tm,),
        in_specs=[
            pl.BlockSpec((tm, kdim), row),
            pl.BlockSpec((kdim, d), const, pipeline_mode=pl.Buffered(1)),
            pl.BlockSpec((tm, d), row),
            pl.BlockSpec((1, d), const),
            pl.BlockSpec((1, d), const),
            pl.BlockSpec((d, ne), const),
            pl.BlockSpec((d, ne), const),
            pl.BlockSpec((1, ne), const),
        ],
        out_specs=[
            pl.BlockSpec((tm, d), row),
            pl.BlockSpec((tm, d // 2), row),
            pl.BlockSpec((tm, LANES), row),
            pl.BlockSpec((tm, LANES), row),
            pl.BlockSpec((1, ne), const),
        ],
        out_shape=[
            jax.ShapeDtypeStruct((t_pad, d), F32),
            jax.ShapeDtypeStruct((t_pad, d // 2), I32),
            jax.ShapeDtypeStruct((t_pad, LANES), I32),
            jax.ShapeDtypeStruct((t_pad, LANES), F32),
            jax.ShapeDtypeStruct((1, ne), F32),
        ],
        scratch_shapes=[pltpu.VMEM((1, ne), F32)],
        compiler_params=_cparams("arbitrary"),
        name="proj_router",
    )(z, w, x, g, b, wr_hi, wr_lo, br)


def _dispatch_kernel(pos_ref, xpk_ref, xs_in_ref, xs_ref, sem, *, n_valid_rows):
    del xs_in_ref
    i = pl.program_id(0)
    tm = xpk_ref.shape[0]
    n_rows = jnp.clip(n_valid_rows - i * tm, 0, tm)

    def row_copy(r, k):
        p = pos_ref[0, 0, r * TOP_K + k]
        return pltpu.make_async_copy(xpk_ref.at[pl.ds(r, 1), :], xs_ref.at[pl.ds(p, 1), :], sem)

    def start(r, c):
        for k in range(TOP_K):
            row_copy(r, k).start()
        return c

    def wait(r, c):
        for k in range(TOP_K):
            row_copy(r, k).wait()
        return c

    lax.fori_loop(0, n_rows, start, 0)
    lax.fori_loop(0, n_rows, wait, 0)


def _dispatch(pos_tiles, xpk, xs_init, *, n_valid_rows):
    t_pad, dh = xpk.shape
    tm = ROW_TILE
    return pl.pallas_call(
        functools.partial(_dispatch_kernel, n_valid_rows=n_valid_rows),
        grid=(t_pad // tm,),
        in_specs=[
            pl.BlockSpec((1, 1, tm * TOP_K), lambda i: (i, 0, 0), memory_space=pltpu.SMEM),
            pl.BlockSpec((tm, dh), lambda i: (i, 0)),
            pl.BlockSpec(memory_space=pl.ANY),
        ],
        out_specs=pl.BlockSpec(memory_space=pl.ANY),
        out_shape=jax.ShapeDtypeStruct(xs_init.shape, xs_init.dtype),
        scratch_shapes=[pltpu.SemaphoreType.DMA(())],
        input_output_aliases={2: 0},
        compiler_params=_cparams("arbitrary"),
        name="moe_dispatch",
    )(pos_tiles, xpk, xs_init)


def _expert_gu_kernel(te_ref, nv_ref, xs_ref, wg_ref, wu_ref, bg_ref, bu_ref, h_ref):
    del te_ref
    i = pl.program_id(1)

    @pl.when(i < nv_ref[0])
    def _():
        xhi, xlo = _unpack_bf16_pairs(xs_ref[...])
        half = xhi.shape[1]
        wg = wg_ref[...].astype(BF16)
        wu = wu_ref[...].astype(BF16)
        g = _dot(xhi, wg[:half]) + _dot(xlo, wg[half:]) + bg_ref[...]
        u = _dot(xhi, wu[:half]) + _dot(xlo, wu[half:]) + bu_ref[...]
        g = jnp.minimum(g, SWIGLU_LIMIT)
        u = jnp.clip(u, -SWIGLU_LIMIT, SWIGLU_LIMIT)
        h_ref[...] = ((u + 1.0) * (g * jax.nn.sigmoid(SWIGLU_ALPHA * g))).astype(h_ref.dtype)

    @pl.when(i >= nv_ref[0])
    def _():
        h_ref[...] = jnp.zeros_like(h_ref)


def _expert_gu(tile_e, n_valid, xs, w_gu, b_gu, layer):
    p_rows, dh = xs.shape
    d = 2 * dh
    dff = w_gu.shape[3] // 2
    tm, tn = EXPERT_TILE, FF_TILE
    nj = dff // tn
    tile = lambda j, i, te, nv: jnp.minimum(i, nv[0] - 1)
    return pl.pallas_call(
        _expert_gu_kernel,
        grid_spec=pltpu.PrefetchScalarGridSpec(
            num_scalar_prefetch=2,
            grid=(nj, p_rows // tm),
            in_specs=[
                pl.BlockSpec((tm, dh), lambda j, i, te, nv: (tile(j, i, te, nv), 0)),
                pl.BlockSpec((None, None, d, tn), lambda j, i, te, nv: (layer, te[i], 0, j)),
                pl.BlockSpec((None, None, d, tn), lambda j, i, te, nv: (layer, te[i], 0, nj + j)),
                pl.BlockSpec((None, None, 1, tn), lambda j, i, te, nv: (layer, te[i], 0, j)),
                pl.BlockSpec((None, None, 1, tn), lambda j, i, te, nv: (layer, te[i], 0, nj + j)),
            ],
            out_specs=pl.BlockSpec((tm, tn), lambda j, i, te, nv: (i, j)),
        ),
        out_shape=jax.ShapeDtypeStruct((p_rows, dff), BF16),
        compiler_params=_cparams("arbitrary", "arbitrary"),
        name="expert_gu",
    )(tile_e, n_valid, xs, w_gu, w_gu, b_gu, b_gu)


def _expert_dn_kernel(te_ref, nv_ref, h_ref, w_ref, b_ref, y_ref):
    del te_ref
    i = pl.program_id(1)

    @pl.when(i < nv_ref[0])
    def _():
        y_ref[...] = _dot(h_ref[...], w_ref[...].astype(BF16)) + b_ref[...]

    @pl.when(i >= nv_ref[0])
    def _():
        y_ref[...] = jnp.zeros_like(y_ref)


def _expert_dn(tile_e, n_valid, h, w_dn, b_dn, layer):
    p_rows, dff = h.shape
    d = w_dn.shape[3]
    tm, tn = EXPERT_TILE, FF_TILE
    nj = d // tn
    tile = lambda j, i, te, nv: jnp.minimum(i, nv[0] - 1)
    return pl.pallas_call(
        _expert_dn_kernel,
        grid_spec=pltpu.PrefetchScalarGridSpec(
            num_scalar_prefetch=2,
            grid=(nj, p_rows // tm),
            in_specs=[
                pl.BlockSpec((tm, dff), lambda j, i, te, nv: (tile(j, i, te, nv), 0)),
                pl.BlockSpec((None, None, dff, tn), lambda j, i, te, nv: (layer, te[i], 0, j)),
                pl.BlockSpec((None, None, 1, tn), lambda j, i, te, nv: (layer, te[i], 0, j)),
            ],
            out_specs=pl.BlockSpec((tm, tn), lambda j, i, te, nv: (i, j)),
        ),
        out_shape=jax.ShapeDtypeStruct((p_rows, d), F32),
        compiler_params=_cparams("arbitrary", "arbitrary"),
        name="expert_dn",
    )(tile_e, n_valid, h, w_dn, b_dn)


def _combine_kernel(pos_ref, wts_ref, x_ref, g_ref, b_ref, ys_ref, xo_ref, xb_ref, buf_ref, sem):
    tm = x_ref.shape[0]

    def row_copy(r, k):
        p = pos_ref[0, 0, r * TOP_K + k]
        return pltpu.make_async_copy(ys_ref.at[pl.ds(p, 1), :], buf_ref.at[k, pl.ds(r, 1), :], sem)

    def start(r, c):
        for k in range(TOP_K):
            row_copy(r, k).start()
        return c

    def wait(r, c):
        for k in range(TOP_K):
            row_copy(r, k).wait()
        return c

    lax.fori_loop(0, tm, start, 0)
    lax.fori_loop(0, tm, wait, 0)

    wts = wts_ref[...]
    f = wts[:, 0:1] * buf_ref[0]
    for k in range(1, TOP_K):
        f = f + wts[:, k:k + 1] * buf_ref[k]
    xn = _layer_norm(ALPHA * x_ref[...] + f, g_ref[...], b_ref[...])
    xo_ref[...] = xn
    xb_ref[...] = xn.astype(BF16)


def _combine(pos_tiles, wts, x, g, b, ys):
    t_pad, d = x.shape
    tm = ROW_TILE
    return pl.pallas_call(
        _combine_kernel,
        grid=(t_pad // tm,),
        in_specs=[
            pl.BlockSpec((1, 1, tm * TOP_K), lambda i: (i, 0, 0), memory_space=pltpu.SMEM),
            pl.BlockSpec((tm, LANES), lambda i: (i, 0)),
            pl.BlockSpec((tm, d), lambda i: (i, 0)),
            pl.BlockSpec((1, d), lambda i: (0, 0)),
            pl.BlockSpec((1, d), lambda i: (0, 0)),
            pl.BlockSpec(memory_space=pl.ANY),
        ],
        out_specs=[
            pl.BlockSpec((tm, d), lambda i: (i, 0)),
            pl.BlockSpec((tm, d), lambda i: (i, 0)),
        ],
        out_shape=[
            jax.ShapeDtypeStruct((t_pad, d), F32),
            jax.ShapeDtypeStruct((t_pad, d), BF16),
        ],
        scratch_shapes=[pltpu.VMEM((TOP_K, tm, d), F32), pltpu.SemaphoreType.DMA(())],
        compiler_params=_cparams("arbitrary"),
        name="moe_combine",
    )(pos_tiles, wts, x, g, b, ys)


def _moe(xo, xpk, meta, wts, counts, ln_g, ln_b, w_gu, b_gu, w_dn, b_dn, layer, *, n_valid_rows):
    t_pad, d = xo.shape
    tm = ROW_TILE
    te = EXPERT_TILE
    n_tiles = -(-(n_valid_rows * TOP_K + N_EXPERTS * (te - 1)) // te)
    cnt = counts[0].astype(I32)
    padded = ((cnt + te - 1) // te) * te
    ends = jnp.cumsum(padded)
    starts = ends - padded
    ids = meta[:, :TOP_K]
    rank = meta[:, TOP_K:2 * TOP_K]
    valid = (jnp.arange(t_pad, dtype=I32) < n_valid_rows)[:, None]
    pos = jnp.where(valid, jnp.take(starts, ids) + rank, 0).astype(I32)
    pos_tiles = pos.reshape(t_pad // tm, 1, tm * TOP_K)
    n_valid_tiles = (ends[-1] // te).astype(I32)
    tile_start = jnp.arange(n_tiles, dtype=I32) * te
    tile_e = jnp.minimum(jnp.sum((ends[None, :] <= tile_start[:, None]).astype(I32), axis=1), N_EXPERTS - 1)
    tile_e = jnp.where(jnp.arange(n_tiles) < n_valid_tiles, tile_e, tile_e[jnp.maximum(n_valid_tiles - 1, 0)])
    nv = n_valid_tiles.reshape(1)

    xs = _dispatch(pos_tiles, xpk, jnp.zeros((n_tiles * te, d // 2), I32), n_valid_rows=n_valid_rows)
    h = _expert_gu(tile_e, nv, xs, w_gu, b_gu, layer)
    ys = _expert_dn(tile_e, nv, h, w_dn, b_dn, layer)
    return _combine(pos_tiles, wts, xo, ln_g, ln_b, ys)


def _kv_proj_kernel(x_ref, w_ref, kf_ref, vf_ref, kb_ref, vb_ref):
    kv = _dot(x_ref[...], w_ref[...])
    half = kv.shape[1] // 2
    k = kv[:, :half]
    v = kv[:, half:]
    kf_ref[...] = k
    vf_ref[...] = v
    kb_ref[...] = k.astype(BF16)
    vb_ref[...] = v.astype(BF16)


def _kv_proj(xb, w_kv):
    t_pad, d = xb.shape
    n = w_kv.shape[1] // 2
    tm = ROW_TILE
    row = lambda i: (i, 0)
    return pl.pallas_call(
        _kv_proj_kernel,
        grid=(t_pad // tm,),
        in_specs=[pl.BlockSpec((tm, d), row),
                  pl.BlockSpec((d, 2 * n), lambda i: (0, 0), pipeline_mode=pl.Buffered(1))],
        out_specs=[pl.BlockSpec((tm, n), row)] * 4,
        out_shape=[jax.ShapeDtypeStruct((t_pad, n), F32), jax.ShapeDtypeStruct((t_pad, n), F32),
                   jax.ShapeDtypeStruct((t_pad, n), BF16), jax.ShapeDtypeStruct((t_pad, n), BF16)],
        compiler_params=_cparams("arbitrary"),
        name="kv_proj",
    )(xb, w_kv)


def _q_proj_kernel(x_ref, w_ref, q_ref):
    q_ref[...] = (_dot(x_ref[...], w_ref[...]) * ATTN_SCALE).astype(BF16)


def _q_proj(xb, w_q):
    t_pad, d = xb.shape
    n = w_q.shape[1]
    tm = ROW_TILE
    return pl.pallas_call(
        _q_proj_kernel,
        grid=(t_pad // tm,),
        in_specs=[pl.BlockSpec((tm, d), lambda i: (i, 0)),
                  pl.BlockSpec((d, n), lambda i: (0, 0), pipeline_mode=pl.Buffered(1))],
        out_specs=pl.BlockSpec((tm, n), lambda i: (i, 0)),
        out_shape=jax.ShapeDtypeStruct((t_pad, n), BF16),
        compiler_params=_cparams("arbitrary"),
        name="q_proj",
    )(xb, w_q)


def _lambda_value(lam_ref, lam_init):
    e1 = jnp.exp(jnp.sum(lam_ref[0:1, :] * lam_ref[1:2, :], axis=1, keepdims=True))
    e2 = jnp.exp(jnp.sum(lam_ref[2:3, :] * lam_ref[3:4, :], axis=1, keepdims=True))
    return e1 - e2 + lam_init


def _bias_from_buckets(bucket, tab_ref, head):
    far = tab_ref[NUM_BUCKETS - 1, head]
    bias = jnp.where(bucket < 0, -jnp.inf, 0.0).astype(F32)
    for b in range(NUM_BUCKETS - 1):
        bias = jnp.where(bucket == b, tab_ref[b, head] - far, bias)
    return bias


def _diff_finalize(acc, l, lam, g, lam_init):
    n = acc.shape[0] // 2
    o = acc[:n] / l[:n] - lam * (acc[n:] / l[n:])
    o = o * lax.rsqrt(jnp.mean(o * o, axis=-1, keepdims=True) + RMS_EPS)
    return o * g * (1.0 - lam_init)


def _prompt_attn_kernel(tab_ref, q_ref, k_ref, v_ref, bkt_ref, lam_ref, g_ref, o_ref,
                        bias_ref, vt_ref, qst_ref, sa_ref, sb_ref, m_ref, l_ref, acc_ref, *, lam_init):
    h = pl.program_id(1)
    qi = pl.program_id(2)
    tq = q_ref.shape[0]
    n_kb, _, tk = vt_ref.shape

    @pl.when(qi == 0)
    def _():
        bias_ref[0] = _bias_from_buckets(bkt_ref[0], tab_ref, h)
        bias_ref[1] = _bias_from_buckets(bkt_ref[1], tab_ref, h)
        bias_ref[2] = jnp.zeros(bias_ref.shape[1:], F32)
        for c in range(n_kb):
            vt_ref[c] = v_ref[c * tk:(c + 1) * tk, :].T.astype(BF16)

    q = q_ref[...].astype(F32)
    lane = lax.broadcasted_iota(I32, q.shape, 1)
    q0 = jnp.where(lane < HEAD_DIM, q, 0.0)
    q1 = jnp.where(lane >= HEAD_DIM, q, 0.0)
    qst_ref[...] = jnp.concatenate([q0.T, q1.T], axis=1).astype(BF16)
    m_ref[...] = jnp.full(m_ref.shape, -jnp.inf, F32)
    l_ref[...] = jnp.zeros(l_ref.shape, F32)
    acc_ref[...] = jnp.zeros(acc_ref.shape, F32)

    def scores(ki, s_ref):
        start = pl.multiple_of(ki * tk, tk)
        s_ref[...] = _dot(k_ref[pl.ds(start, tk), :], qst_ref[...])

    def consume(ki, s_ref):
        bias = bias_ref[jnp.minimum(qi - ki, 2)]
        s = s_ref[...] + jnp.concatenate([bias, bias], axis=1)
        m_old = m_ref[...]
        m_new = jnp.maximum(m_old, jnp.max(s, axis=0, keepdims=True))
        p = jnp.exp(s - m_new)
        corr = jnp.exp(m_old - m_new)
        l_ref[...] = corr * l_ref[...] + jnp.sum(p, axis=0, keepdims=True)
        acc_ref[...] = corr * acc_ref[...] + _dot(vt_ref[ki], p.astype(BF16))
        m_ref[...] = m_new

    scores(0, sa_ref)

    def pair(j, carry):
        k0 = 2 * j
        scores(k0 + 1, sb_ref)
        consume(k0, sa_ref)
        scores(jnp.minimum(k0 + 2, qi), sa_ref)
        consume(k0 + 1, sb_ref)
        return carry

    n_blocks = qi + 1
    lax.fori_loop(0, n_blocks // 2, pair, 0)

    @pl.when(n_blocks % 2 == 1)
    def _():
        consume(qi, sa_ref)

    lam = _lambda_value(lam_ref, lam_init)
    on = acc_ref[...] / l_ref[...]
    o = (on[:, :tq] - lam * on[:, tq:]).T
    o = o * lax.rsqrt(jnp.mean(o * o, axis=-1, keepdims=True) + RMS_EPS)
    o_ref[...] = (o * g_ref[...] * (1.0 - lam_init)).astype(o_ref.dtype)


def _prompt_attn(q, kb, vf, buckets_t, table, lam_vecs, g, *, batch, seq, lam_init):
    tq, tk = ATTN_Q_TILE, ATTN_K_TILE
    nq = seq // tq
    return pl.pallas_call(
        functools.partial(_prompt_attn_kernel, lam_init=lam_init),
        grid_spec=pltpu.PrefetchScalarGridSpec(
            num_scalar_prefetch=1,
            grid=(batch, N_HEADS, nq),
            in_specs=[
                pl.BlockSpec((tq, V_DIM), lambda b, h, i, tab: (b * nq + i, h)),
                pl.BlockSpec((seq, V_DIM), lambda b, h, i, tab: (b, h)),
                pl.BlockSpec((seq, V_DIM), lambda b, h, i, tab: (b, h)),
                pl.BlockSpec((2, tk, tq), lambda b, h, i, tab: (0, 0, 0)),
                pl.BlockSpec((4, HEAD_DIM), lambda b, h, i, tab: (0, 0)),
                pl.BlockSpec((1, V_DIM), lambda b, h, i, tab: (0, 0)),
            ],
            out_specs=pl.BlockSpec((tq, V_DIM), lambda b, h, i, tab: (b * nq + i, h)),
            scratch_shapes=[
                pltpu.VMEM((3, tk, tq), F32),
                pltpu.VMEM((seq // tk, V_DIM, tk), BF16),
                pltpu.VMEM((V_DIM, 2 * tq), BF16),
                pltpu.VMEM((tk, 2 * tq), F32),
                pltpu.VMEM((tk, 2 * tq), F32),
                pltpu.VMEM((1, 2 * tq), F32),
                pltpu.VMEM((1, 2 * tq), F32),
                pltpu.VMEM((V_DIM, 2 * tq), F32),
            ],
        ),
        out_shape=jax.ShapeDtypeStruct((batch * seq, N_HEADS * V_DIM), BF16),
        compiler_params=_cparams("arbitrary", "arbitrary", "arbitrary"),
        name="prompt_attn",
    )(table, q, kb, vf, buckets_t, lam_vecs, g)


def _sample_attn_kernel(pt_ref, tab_ref, q_ref, kn_ref, vn_ref, bkt_last_ref, bkt_new_ref, lam_ref, g_ref, *refs,
                        lam_init, n_pages_step):
    del pt_ref
    kt_refs = refs[:n_pages_step]
    v_refs = refs[n_pages_step:2 * n_pages_step]
    o_ref, qbd_ref, m_ref, l_ref, acc_ref, bias_ref = refs[2 * n_pages_step:]
    c = pl.program_id(1)
    nc = pl.num_programs(1)
    lq = q_ref.shape[1]
    rows_h = 2 * lq
    page = kt_refs[0].shape[1]

    @pl.when(c == 0)
    def _():
        qbd_ref[...] = jnp.zeros(qbd_ref.shape, qbd_ref.dtype)
        for h in range(N_HEADS):
            q = q_ref[0, :, h * V_DIM:(h + 1) * V_DIM]
            lane = lax.broadcasted_iota(I32, q.shape, 1)
            zero = jnp.zeros_like(q)
            qs = jnp.concatenate([jnp.where(lane < HEAD_DIM, q, zero), jnp.where(lane >= HEAD_DIM, q, zero)], axis=0)
            qbd_ref[h * rows_h:(h + 1) * rows_h, h * V_DIM:(h + 1) * V_DIM] = qs.astype(BF16)
        m_ref[...] = jnp.full(m_ref.shape, -jnp.inf, F32)
        l_ref[...] = jnp.zeros(l_ref.shape, F32)
        acc_ref[...] = jnp.zeros(acc_ref.shape, F32)
        bias_ref[...] = jnp.zeros(bias_ref.shape, F32)

    def stacked_bias(bucket):
        tiles = []
        for h in range(N_HEADS):
            b = _bias_from_buckets(bucket, tab_ref, h)
            tiles += [b, b]
        return jnp.concatenate(tiles, axis=0)

    @pl.when(c == nc - 1)
    def _():
        bias_ref[...] = stacked_bias(bkt_last_ref[...])

    def update(s, vmat):
        m_old = m_ref[...]
        m_new = jnp.maximum(m_old, jnp.max(s, axis=1, keepdims=True))
        p = jnp.exp(s - m_new)
        corr = jnp.exp(m_old - m_new)
        l_ref[...] = corr * l_ref[...] + jnp.sum(p, axis=1, keepdims=True)
        r = _dot(p.astype(BF16), vmat)
        pv = jnp.concatenate(
            [r[h * rows_h:(h + 1) * rows_h, h * V_DIM:(h + 1) * V_DIM] for h in range(N_HEADS)], axis=0)
        acc_ref[...] = corr * acc_ref[...] + pv
        m_ref[...] = m_new

    qbd = qbd_ref[...]
    s = jnp.concatenate([_dot(qbd, kt[...].astype(BF16)) for kt in kt_refs], axis=1) + bias_ref[...]
    vmat = jnp.concatenate(
        [jnp.concatenate([vr[pl.ds(h, page, stride=N_HEADS), :] for h in range(N_HEADS)], axis=1).astype(BF16)
         for vr in v_refs], axis=0)
    update(s, vmat)

    @pl.when(c == nc - 1)
    def _():
        update(_dot_nt(qbd, kn_ref[0]) + stacked_bias(bkt_new_ref[...]), vn_ref[0])
        lam = _lambda_value(lam_ref, lam_init)
        for h in range(N_HEADS):
            rows = slice(h * rows_h, (h + 1) * rows_h)
            o = _diff_finalize(acc_ref[rows, :], l_ref[rows, :], lam, g_ref[...], lam_init)
            o_ref[0, :, h * V_DIM:(h + 1) * V_DIM] = o.astype(o_ref.dtype)


def _sample_attn(q_s, k_new, v_new, cache_kt, cache_vr, page_table, bkt_last, bkt_new, table, lam_vecs, g, *,
                 lam_init):
    bsz, lq, dm = q_s.shape
    page = cache_kt.shape[2]
    n_pages = page_table.shape[1]
    pg = DEC_PAGES_PER_STEP
    nc = n_pages // pg
    rows = N_HEADS * 2 * lq

    def page_spec(shape, r):
        return pl.BlockSpec((None,) + shape, lambda b, c, pt, tab: (pt[b, c * pg + r], 0, 0))

    per_b = lambda b, c, pt, tab: (b, 0, 0)
    const2 = lambda b, c, pt, tab: (0, 0)
    return pl.pallas_call(
        functools.partial(_sample_attn_kernel, lam_init=lam_init, n_pages_step=pg),
        grid_spec=pltpu.PrefetchScalarGridSpec(
            num_scalar_prefetch=2,
            grid=(bsz, nc),
            in_specs=[
                pl.BlockSpec((1, lq, dm), per_b),
                pl.BlockSpec((1, page, dm), per_b),
                pl.BlockSpec((1, page, dm), per_b),
                pl.BlockSpec((lq, pg * page), const2),
                pl.BlockSpec((lq, page), const2),
                pl.BlockSpec((4, HEAD_DIM), const2),
                pl.BlockSpec((1, V_DIM), const2),
            ] + [page_spec((dm, page), r) for r in range(pg)]
              + [page_spec((page * N_HEADS, V_DIM), r) for r in range(pg)],
            out_specs=pl.BlockSpec((1, lq, dm), per_b),
            scratch_shapes=[
                pltpu.VMEM((rows, dm), BF16),
                pltpu.VMEM((rows, 1), F32),
                pltpu.VMEM((rows, 1), F32),
                pltpu.VMEM((rows, V_DIM), F32),
                pltpu.VMEM((rows, pg * page), F32),
            ],
        ),
        out_shape=jax.ShapeDtypeStruct((bsz, lq, dm), F32),
        compiler_params=_cparams("arbitrary", "arbitrary"),
        name="sample_attn",
    )(page_table, table, q_s, k_new, v_new, bkt_last, bkt_new, lam_vecs, g, *([cache_kt] * pg), *([cache_vr] * pg))


def _t5_bucket(n):
    max_exact = NUM_BUCKETS // 2
    nf = jnp.maximum(n, 1).astype(F32)
    large = max_exact + (jnp.log(nf / max_exact) / math.log(MAX_DISTANCE / max_exact)
                         * (NUM_BUCKETS - max_exact)).astype(I32)
    large = jnp.minimum(large, NUM_BUCKETS - 1)
    return jnp.where(n < 0, -1, jnp.where(n < max_exact, n, large)).astype(I32)


def kernel(x_prompt, x_sample, state_conv, cache_k, cache_v, page_table, conv_w_in, conv_w, conv_w_out, attn_w_q,
           attn_w_kv, lambda_q1, lambda_k1, lambda_q2, lambda_k2, subln_g, attn_w_o, rel_bias, ln_mix_g, ln_mix_b,
           ln_ffn_g, ln_ffn_b, router_w, router_b, expert_w_gu, expert_b_gu, expert_w_dn, expert_b_dn):
    bp, lp, d = x_prompt.shape
    bs, ls, _ = x_sample.shape
    n_pool, page = cache_k.shape[0], cache_k.shape[1]
    n_pages = page_table.shape[1]
    tp, ts = bp * lp, bs * ls
    t_valid = tp + ts
    tm = ROW_TILE
    assert lp % tm == 0 and ts <= tm and lp % ATTN_Q_TILE == 0 and ls >= CONV_WIDTH - 1
    assert n_pages % DEC_PAGES_PER_STEP == 0 and page >= ls
    n_prompt_tiles = tp // tm
    t_pad = tp + tm
    assert ATTN_K_TILE >= MAX_DISTANCE and DEC_PAGES_PER_STEP * page >= MAX_DISTANCE

    x = jnp.concatenate([x_prompt.reshape(tp, d), x_sample.reshape(ts, d), jnp.zeros((t_pad - t_valid, d), F32)], axis=0)
    xb = x.astype(BF16)

    tq, tk = ATTN_Q_TILE, ATTN_K_TILE
    rq = jnp.arange(tq, dtype=I32)[:, None]
    ck = jnp.arange(tk, dtype=I32)[None, :]
    bkt_prompt = jnp.stack([_t5_bucket(rq - ck).T, _t5_bucket(tk + rq - ck).T])
    rs = jnp.arange(ls, dtype=I32)[:, None]
    chunk = DEC_PAGES_PER_STEP * page
    bkt_last = _t5_bucket(chunk + rs - jnp.arange(chunk, dtype=I32)[None, :])
    bkt_new = _t5_bucket(rs - jnp.arange(page, dtype=I32)[None, :])

    st_prompt = jnp.zeros((bp, CONV_WIDTH - 1, d), F32)
    conv_states_p, conv_states_s = [], []
    k_f = v_f = k_b = v_b = None
    cache_kt = jnp.transpose(cache_k, (0, 2, 3, 4, 1)).reshape(n_pool, d, page)
    cache_vr = cache_v.reshape(n_pool, page * N_HEADS, V_DIM)

    for l in range(DEPTH):
        if l < N_CONV_LAYERS:
            z, st_p, st_s = _conv_in(xb, conv_w_in[l].astype(BF16), st_prompt, state_conv[l], conv_w[l],
                                     n_prompt_tiles=n_prompt_tiles, tiles_per_seq=lp // tm, dec_batch=bs, dec_seq=ls)
            conv_states_p.append(st_p)
            conv_states_s.append(st_s)
            w_out = conv_w_out[l].astype(BF16)
        else:
            j = l - N_CONV_LAYERS
            if j == 0:
                k_f, v_f, k_b, v_b = _kv_proj(xb, attn_w_kv.astype(BF16))
                k_new = jnp.zeros((bs, page, d), BF16).at[:, :ls].set(k_b[tp:t_valid].reshape(bs, ls, d))
                v_new = jnp.zeros((bs, page, d), BF16).at[:, :ls].set(v_b[tp:t_valid].reshape(bs, ls, d))
            lam_init = 0.8 - 0.6 * math.exp(-0.3 * l)
            lam_vecs = jnp.stack([lambda_q1[j], lambda_k1[j], lambda_q2[j], lambda_k2[j]])
            g = subln_g[j].reshape(1, V_DIM)
            q = _q_proj(xb, attn_w_q[j].astype(BF16))
            o_p = _prompt_attn(q, k_b, v_f, bkt_prompt, rel_bias, lam_vecs, g, batch=bp, seq=lp, lam_init=lam_init)
            o_s = _sample_attn(q[tp:t_valid].astype(F32).reshape(bs, ls, d), k_new, v_new, cache_kt, cache_vr,
                               page_table, bkt_last, bkt_new, rel_bias, lam_vecs, g, lam_init=lam_init)
            z = jnp.concatenate([o_p, o_s.astype(BF16).reshape(ts, d), jnp.zeros((t_pad - t_valid, d), BF16)], axis=0)
            w_out = attn_w_o[j].astype(BF16)

        wr = router_w[l]
        wr_hi = wr.astype(BF16)
        wr_lo = (wr - wr_hi.astype(F32)).astype(BF16)
        xo, xpk, meta, wts, counts = _proj_router(
            z, w_out, x, ln_mix_g[l].reshape(1, d), ln_mix_b[l].reshape(1, d), wr_hi, wr_lo,
            router_b[l].reshape(1, N_EXPERTS), n_valid_rows=t_valid)
        x, xb = _moe(xo, xpk, meta, wts, counts, ln_ffn_g[l].reshape(1, d), ln_ffn_b[l].reshape(1, d),
                     expert_w_gu, expert_b_gu.reshape(DEPTH, N_EXPERTS, 1, -1), expert_w_dn,
                     expert_b_dn.reshape(DEPTH, N_EXPERTS, 1, -1), l, n_valid_rows=t_valid)

    hk = (N_HEADS, 2, HEAD_DIM)
    hv = (N_HEADS, V_DIM)
    return (x[:tp].reshape(bp, lp, d), x[tp:t_valid].reshape(bs, ls, d), jnp.stack(conv_states_p),
            k_f[:tp].reshape(bp, lp, *hk), v_f[:tp].reshape(bp, lp, *hv), jnp.stack(conv_states_s),
            k_f[tp:t_valid].reshape(bs, ls, *hk), v_f[tp:t_valid].reshape(bs, ls, *hv))
```

```python
import functools
import math

import jax
import jax.numpy as jnp
from jax import lax
from jax.experimental import pallas as pl
from jax.experimental.pallas import tpu as pltpu

F32 = jnp.float32
BF16 = jnp.bfloat16
I32 = jnp.int32

DEPTH = 4
N_CONV_LAYERS = 2
CONV_WIDTH = 3
N_HEADS = 16
HEAD_DIM = 64
V_DIM = 128
NUM_BUCKETS = 32
MAX_DISTANCE = 128
N_EXPERTS = 32
TOP_K = 4
SWIGLU_LIMIT = 7.0
SWIGLU_ALPHA = 1.702
LN_EPS = 1e-5
RMS_EPS = 1e-5
ALPHA = (2 * DEPTH) ** 0.25
ATTN_SCALE = HEAD_DIM ** -0.5

ROW_TILE = 256
EXPERT_TILE = 256
FF_TILE = 1024
CONV_COL_TILE = 512
ATTN_Q_TILE = 512
ATTN_K_TILE = 256
DEC_PAGES_PER_STEP = 4
DEC_HEAD_GROUP = 4
VMEM_LIMIT = 56 * 1024 * 1024
LANES = 128
DMA_LOOP_UNROLL = 8


def _cparams(*sem):
    return pltpu.CompilerParams(dimension_semantics=sem, vmem_limit_bytes=VMEM_LIMIT)


def _dot(a, b):
    return jnp.dot(a, b, preferred_element_type=F32)


def _dot_nt(a, b):
    return lax.dot_general(a, b, (((1,), (1,)), ((), ())), preferred_element_type=F32)


def _layer_norm(r, g, b):
    mu = jnp.mean(r, axis=-1, keepdims=True)
    xc = r - mu
    var = jnp.mean(xc * xc, axis=-1, keepdims=True)
    return xc * lax.rsqrt(var + LN_EPS) * g + b


def _pack_bf16_pairs(xn):
    half = xn.shape[1] // 2
    hi = lax.bitcast_convert_type(xn[:, :half].astype(BF16).astype(F32), I32)
    lo = lax.bitcast_convert_type(xn[:, half:].astype(BF16).astype(F32), I32)
    return (hi & jnp.int32(-65536)) | lax.shift_right_logical(lo, jnp.int32(16))


def _unpack_bf16_pairs(pk):
    hi = lax.bitcast_convert_type(pk & jnp.int32(-65536), F32).astype(BF16)
    lo = lax.bitcast_convert_type(lax.shift_left(pk, jnp.int32(16)), F32).astype(BF16)
    return hi, lo


def _conv_in_kernel(x_ref, wb_ref, wc_ref, wh_ref, stp_ref, sts_ref, cw_ref,
                    z_ref, sop_ref, sos_ref, carry_ref, *, n_prompt_tiles, tiles_per_seq, dec_batch, dec_seq):
    i = pl.program_id(1)
    x = x_ref[...]
    bg = _dot(x, wb_ref[...])
    u = _dot(x, wc_ref[...]) * _dot(x, wh_ref[...])
    tm, tn = u.shape
    row = lax.broadcasted_iota(I32, (tm, tn), 0)
    s1 = pltpu.roll(u, 1, axis=0)
    s2 = pltpu.roll(u, 2, axis=0)
    cw = cw_ref[...]

    def finish(s1f, s2f):
        v = cw[0:1, :] * s2f + cw[1:2, :] * s1f + cw[2:3, :] * u
        z_ref[...] = (bg * v).astype(z_ref.dtype)

    @pl.when(i < n_prompt_tiles)
    def _():
        seq_start = (i % tiles_per_seq) == 0
        p1 = jnp.where(seq_start, stp_ref[0, 1:2, :], carry_ref[1:2, :])
        p2 = jnp.where(seq_start, stp_ref[0, 0:1, :], carry_ref[0:1, :])
        s1f = jnp.where(row == 0, p1, s1)
        s2f = jnp.where(row == 0, p2, jnp.where(row == 1, p1, s2))
        finish(s1f, s2f)
        tail = u[tm - 2:tm, :]
        carry_ref[0:2, :] = tail
        sop_ref[0] = tail

    @pl.when(i == n_prompt_tiles)
    def _():
        s1f, s2f = s1, s2
        for s in range(dec_batch):
            r0 = s * dec_seq
            p1 = sts_ref[s, 1:2, :]
            p2 = sts_ref[s, 0:1, :]
            s1f = jnp.where(row == r0, p1, s1f)
            s2f = jnp.where(row == r0, p2, jnp.where(row == r0 + 1, p1, s2f))
            sos_ref[s] = u[r0 + dec_seq - 2:r0 + dec_seq, :]
        finish(s1f, s2f)


def _conv_in(xb, w_in, st_prompt, st_sample, conv_w, *, n_prompt_tiles, tiles_per_seq, dec_batch, dec_seq):
    t_pad, d = xb.shape
    tm, tn = ROW_TILE, CONV_COL_TILE
    nb = d // tn
    n_seq_p = st_prompt.shape[0]
    grid = (nb, t_pad // tm)
    seq_of = lambda i: jnp.minimum(i // tiles_per_seq, n_seq_p - 1)
    kern = functools.partial(_conv_in_kernel, n_prompt_tiles=n_prompt_tiles, tiles_per_seq=tiles_per_seq,
                             dec_batch=dec_batch, dec_seq=dec_seq)
    return pl.pallas_call(
        kern,
        grid=grid,
        in_specs=[
            pl.BlockSpec((tm, d), lambda j, i: (i, 0)),
            pl.BlockSpec((d, tn), lambda j, i: (0, j)),
            pl.BlockSpec((d, tn), lambda j, i: (0, nb + j)),
            pl.BlockSpec((d, tn), lambda j, i: (0, 2 * nb + j)),
            pl.BlockSpec((1, CONV_WIDTH - 1, tn), lambda j, i: (seq_of(i), 0, j)),
            pl.BlockSpec((dec_batch, CONV_WIDTH - 1, tn), lambda j, i: (0, 0, j)),
            pl.BlockSpec((CONV_WIDTH, tn), lambda j, i: (0, j)),
        ],
        out_specs=[
            pl.BlockSpec((tm, tn), lambda j, i: (i, j)),
            pl.BlockSpec((1, CONV_WIDTH - 1, tn), lambda j, i: (seq_of(i), 0, j)),
            pl.BlockSpec((dec_batch, CONV_WIDTH - 1, tn), lambda j, i: (0, 0, j)),
        ],
        out_shape=[
            jax.ShapeDtypeStruct((t_pad, d), BF16),
            jax.ShapeDtypeStruct(st_prompt.shape, F32),
            jax.ShapeDtypeStruct(st_sample.shape, F32),
        ],
        scratch_shapes=[pltpu.VMEM((8, tn), F32)],
        compiler_params=_cparams("arbitrary", "arbitrary"),
        name="conv_in",
    )(xb, w_in, w_in, w_in, st_prompt, st_sample, conv_w)


def _proj_router_kernel(z_ref, w_ref, x_ref, g_ref, b_ref, wrh_ref, wrl_ref, br_ref,
                        xo_ref, xpk_ref, meta_ref, wts_ref, cnt_ref, carry_ref, *, n_valid_rows):
    i = pl.program_id(0)
    tm = z_ref.shape[0]

    @pl.when(i == 0)
    def _():
        carry_ref[...] = jnp.zeros_like(carry_ref)

    y = _dot(z_ref[...], w_ref[...])
    xn = _layer_norm(ALPHA * x_ref[...] + y, g_ref[...], b_ref[...])
    xo_ref[...] = xn
    xpk_ref[...] = _pack_bf16_pairs(xn)

    xh = xn.astype(BF16)
    xl = (xn - xh.astype(F32)).astype(BF16)
    wrh = wrh_ref[...]
    logits = _dot(xh, wrh) + (_dot(xl, wrh) + _dot(xh, wrl_ref[...])) + br_ref[...]

    ne = logits.shape[1]
    lane = lax.broadcasted_iota(I32, (tm, ne), 1)
    lane_f = lane.astype(F32)
    work = logits
    ids, vals = [], []
    for _ in range(TOP_K):
        m = jnp.max(work, axis=1, keepdims=True)
        idx = jnp.min(jnp.where(work == m, lane_f, float(ne)), axis=1, keepdims=True).astype(I32)
        ids.append(idx)
        vals.append(m)
        work = jnp.where(lane == idx, -jnp.inf, work)
    exps = [jnp.exp(v - vals[0]) for v in vals]
    denom = exps[0] + exps[1] + exps[2] + exps[3]

    grow = i * tm + lax.broadcasted_iota(I32, (tm, 1), 0)
    valid = grow < n_valid_rows
    onehots = [jnp.logical_and(lane == idx, valid) for idx in ids]
    sel = onehots[0] | onehots[1] | onehots[2] | onehots[3]
    selb = jnp.where(sel, 1.0, 0.0).astype(BF16)
    tri = (lax.broadcasted_iota(I32, (tm, tm), 0) >= lax.broadcasted_iota(I32, (tm, tm), 1)).astype(BF16)
    incl = _dot(tri, selb)
    excl = incl - selb.astype(F32) + carry_ref[...]

    lane128 = lax.broadcasted_iota(I32, (tm, LANES), 1)
    meta = jnp.zeros((tm, LANES), I32)
    wts = jnp.zeros((tm, LANES), F32)
    for k in range(TOP_K):
        rank = jnp.sum(jnp.where(onehots[k], excl, 0.0), axis=1, keepdims=True).astype(I32)
        meta = jnp.where(lane128 == k, ids[k], meta)
        meta = jnp.where(lane128 == TOP_K + k, rank, meta)
        wts = jnp.where(lane128 == k, jnp.where(valid, exps[k] / denom, 0.0), wts)
    meta_ref[...] = meta
    wts_ref[...] = wts
    new_cnt = carry_ref[...] + incl[tm - 1:tm, :]
    carry_ref[...] = new_cnt
    cnt_ref[...] = new_cnt


def _proj_router(z, w, x, g, b, wr_hi, wr_lo, br, *, n_valid_rows):
    t_pad, d = x.shape
    kdim = z.shape[1]
    tm = ROW_TILE
    ne = wr_hi.shape[1]
    const = lambda i: (0, 0)
    row = lambda i: (i, 0)
    return pl.pallas_call(
        functools.partial(_proj_router_kernel, n_valid_rows=n_valid_rows),
        grid=(t_pad // tm,),
        in_specs=[
            pl.BlockSpec((tm, kdim), row),
            pl.BlockSpec((kdim, d), const, pipeline_mode=pl.Buffered(1)),
            pl.BlockSpec((tm, d), row),
            pl.BlockSpec((1, d), const),
            pl.BlockSpec((1, d), const),
            pl.BlockSpec((d, ne), const),
            pl.BlockSpec((d, ne), const),
            pl.BlockSpec((1, ne), const),
        ],
        out_specs=[
            pl.BlockSpec((tm, d), row),
            pl.BlockSpec((tm, d // 2), row),
            pl.BlockSpec((tm, LANES), row),
            pl.BlockSpec((tm, LANES), row),
            pl.BlockSpec((1, ne), const),
        ],
        out_shape=[
            jax.ShapeDtypeStruct((t_pad, d), F32),
            jax.ShapeDtypeStruct((t_pad, d // 2), I32),
            jax.ShapeDtypeStruct((t_pad, LANES), I32),
            jax.ShapeDtypeStruct((t_pad, LANES), F32),
            jax.ShapeDtypeStruct((1, ne), F32),
        ],
        scratch_shapes=[pltpu.VMEM((1, ne), F32)],
        compiler_params=_cparams("arbitrary"),
        name="proj_router",
    )(z, w, x, g, b, wr_hi, wr_lo, br)


def _dispatch_kernel(pos_ref, xpk_ref, xs_in_ref, xs_ref, sem, *, n_valid_rows):
    del xs_in_ref
    i = pl.program_id(0)
    tm = xpk_ref.shape[0]
    n_full, n_rest = divmod(n_valid_rows, tm)

    def row_copy(r, k):
        p = pos_ref[0, 0, r * TOP_K + k]
        return pltpu.make_async_copy(xpk_ref.at[pl.ds(r, 1), :], xs_ref.at[pl.ds(p, 1), :], sem)

    def start(r, c):
        for k in range(TOP_K):
            row_copy(r, k).start()
        return c

    def wait(r, c):
        for k in range(TOP_K):
            row_copy(r, k).wait()
        return c

    def scatter_rows(n_rows):
        lax.fori_loop(0, n_rows, start, 0, unroll=DMA_LOOP_UNROLL)
        lax.fori_loop(0, n_rows, wait, 0, unroll=DMA_LOOP_UNROLL)

    @pl.when(i < n_full)
    def _():
        scatter_rows(tm)

    if n_rest:
        @pl.when(i == n_full)
        def _():
            scatter_rows(n_rest)


def _dispatch(pos_tiles, xpk, xs_init, *, n_valid_rows):
    t_pad, dh = xpk.shape
    tm = ROW_TILE
    return pl.pallas_call(
        functools.partial(_dispatch_kernel, n_valid_rows=n_valid_rows),
        grid=(t_pad // tm,),
        in_specs=[
            pl.BlockSpec((1, 1, tm * TOP_K), lambda i: (i, 0, 0), memory_space=pltpu.SMEM),
            pl.BlockSpec((tm, dh), lambda i: (i, 0)),
            pl.BlockSpec(memory_space=pl.ANY),
        ],
        out_specs=pl.BlockSpec(memory_space=pl.ANY),
        out_shape=jax.ShapeDtypeStruct(xs_init.shape, xs_init.dtype),
        scratch_shapes=[pltpu.SemaphoreType.DMA(())],
        input_output_aliases={2: 0},
        compiler_params=_cparams("arbitrary"),
        name="moe_dispatch",
    )(pos_tiles, xpk, xs_init)


def _run_weights(te_ref, first_ref, run_ref, nxt_ref, nv_ref, copies, n_col_passes):
    j = pl.program_id(0)
    i = pl.program_id(1)
    n_runs = nv_ref[1]
    g = j * n_runs + run_ref[i]
    slot = g % 2

    @pl.when(first_ref[i] == 1)
    def _():
        @pl.when(g == 0)
        def _():
            for c in copies(j, te_ref[i], slot):
                c.start()

        for c in copies(j, te_ref[i], slot):
            c.wait()
        last_run = run_ref[i] == n_runs - 1

        @pl.when(jnp.logical_not(jnp.logical_and(last_run, j == n_col_passes - 1)))
        def _():
            for c in copies(jnp.where(last_run, j + 1, j), nxt_ref[i], 1 - slot):
                c.start()

    return slot


def _expert_gu_kernel(te_ref, first_ref, run_ref, nxt_ref, nv_ref, xs_ref, bg_ref, bu_ref, w_ref, h_ref, wbuf, sem,
                      *, layer):
    i = pl.program_id(1)
    nj = pl.num_programs(0)
    tn = wbuf.shape[-1]

    def copies(jj, e, s):
        return [pltpu.make_async_copy(w_ref.at[layer, e, :, pl.ds(pl.multiple_of((n * nj + jj) * tn, tn), tn)],
                                      wbuf.at[s, n], sem.at[s, n]) for n in range(2)]

    @pl.when(i < nv_ref[0])
    def _():
        slot = _run_weights(te_ref, first_ref, run_ref, nxt_ref, nv_ref, copies, nj)
        xhi, xlo = _unpack_bf16_pairs(xs_ref[...])
        half = xhi.shape[1]
        wg = wbuf[slot, 0].astype(BF16)
        wu = wbuf[slot, 1].astype(BF16)
        g = _dot(xhi, wg[:half]) + _dot(xlo, wg[half:]) + bg_ref[...]
        u = _dot(xhi, wu[:half]) + _dot(xlo, wu[half:]) + bu_ref[...]
        g = jnp.minimum(g, SWIGLU_LIMIT)
        u = jnp.clip(u, -SWIGLU_LIMIT, SWIGLU_LIMIT)
        h_ref[...] = ((u + 1.0) * (g * jax.nn.sigmoid(SWIGLU_ALPHA * g))).astype(h_ref.dtype)

    @pl.when(i >= nv_ref[0])
    def _():
        h_ref[...] = jnp.zeros_like(h_ref)


def _expert_gu(sched, xs, w_gu, b_gu, layer):
    p_rows, dh = xs.shape
    d = 2 * dh
    dff = w_gu.shape[3] // 2
    tm, tn = EXPERT_TILE, FF_TILE
    nj = dff // tn
    tile = lambda j, i, te, fi, ru, nx, nv: jnp.minimum(i, nv[0] - 1)
    return pl.pallas_call(
        functools.partial(_expert_gu_kernel, layer=layer),
        grid_spec=pltpu.PrefetchScalarGridSpec(
            num_scalar_prefetch=5,
            grid=(nj, p_rows // tm),
            in_specs=[
                pl.BlockSpec((tm, dh), lambda j, i, te, fi, ru, nx, nv: (tile(j, i, te, fi, ru, nx, nv), 0)),
                pl.BlockSpec((None, None, 1, tn), lambda j, i, te, fi, ru, nx, nv: (layer, te[i], 0, j)),
                pl.BlockSpec((None, None, 1, tn), lambda j, i, te, fi, ru, nx, nv: (layer, te[i], 0, nj + j)),
                pl.BlockSpec(memory_space=pl.ANY),
            ],
            out_specs=pl.BlockSpec((tm, tn), lambda j, i, te, fi, ru, nx, nv: (i, j)),
            scratch_shapes=[pltpu.VMEM((2, 2, d, tn), F32), pltpu.SemaphoreType.DMA((2, 2))],
        ),
        out_shape=jax.ShapeDtypeStruct((p_rows, dff), BF16),
        compiler_params=_cparams("arbitrary", "arbitrary"),
        name="expert_gu",
    )(*sched, xs, b_gu, b_gu, w_gu)


def _expert_dn_kernel(te_ref, first_ref, run_ref, nxt_ref, nv_ref, h_ref, b_ref, w_ref, y_ref, wbuf, sem, *, layer):
    i = pl.program_id(1)
    nj = pl.num_programs(0)
    tn = wbuf.shape[-1]

    def copies(jj, e, s):
        return [pltpu.make_async_copy(w_ref.at[layer, e, :, pl.ds(pl.multiple_of(jj * tn, tn), tn)],
                                      wbuf.at[s], sem.at[s])]

    @pl.when(i < nv_ref[0])
    def _():
        slot = _run_weights(te_ref, first_ref, run_ref, nxt_ref, nv_ref, copies, nj)
        y_ref[...] = _dot(h_ref[...], wbuf[slot].astype(BF16)) + b_ref[...]

    @pl.when(i >= nv_ref[0])
    def _():
        y_ref[...] = jnp.zeros_like(y_ref)


def _expert_dn(sched, h, w_dn, b_dn, layer):
    p_rows, dff = h.shape
    d = w_dn.shape[3]
    tm, tn = EXPERT_TILE, FF_TILE
    nj = d // tn
    tile = lambda j, i, te, fi, ru, nx, nv: jnp.minimum(i, nv[0] - 1)
    return pl.pallas_call(
        functools.partial(_expert_dn_kernel, layer=layer),
        grid_spec=pltpu.PrefetchScalarGridSpec(
            num_scalar_prefetch=5,
            grid=(nj, p_rows // tm),
            in_specs=[
                pl.BlockSpec((tm, dff), lambda j, i, te, fi, ru, nx, nv: (tile(j, i, te, fi, ru, nx, nv), 0)),
                pl.BlockSpec((None, None, 1, tn), lambda j, i, te, fi, ru, nx, nv: (layer, te[i], 0, j)),
                pl.BlockSpec(memory_space=pl.ANY),
            ],
            out_specs=pl.BlockSpec((tm, tn), lambda j, i, te, fi, ru, nx, nv: (i, j)),
            scratch_shapes=[pltpu.VMEM((2, dff, tn), F32), pltpu.SemaphoreType.DMA((2,))],
        ),
        out_shape=jax.ShapeDtypeStruct((p_rows, d), F32),
        compiler_params=_cparams("arbitrary", "arbitrary"),
        name="expert_dn",
    )(*sched, h, b_dn, w_dn)


def _combine_kernel(pos_ref, posn_ref, wts_ref, x_ref, g_ref, b_ref, ys_ref, xo_ref, xb_ref, buf_ref, sem):
    i = pl.program_id(0)
    n = pl.num_programs(0)
    tm = x_ref.shape[0]

    def row_copy(p_ref, slot, r, k):
        p = p_ref[0, 0, r * TOP_K + k]
        return pltpu.make_async_copy(ys_ref.at[pl.ds(p, 1), :], buf_ref.at[slot, k, pl.ds(r, 1), :], sem.at[slot])

    def start_tile(p_ref, slot):
        def body(r, c):
            for k in range(TOP_K):
                row_copy(p_ref, slot, r, k).start()
            return c
        lax.fori_loop(0, tm, body, 0, unroll=DMA_LOOP_UNROLL)

    def wait_tile(p_ref, slot):
        def body(r, c):
            for k in range(TOP_K):
                row_copy(p_ref, slot, r, k).wait()
            return c
        lax.fori_loop(0, tm, body, 0, unroll=DMA_LOOP_UNROLL)

    slot = i % 2

    @pl.when(i == 0)
    def _():
        start_tile(pos_ref, 0)

    @pl.when(i + 1 < n)
    def _():
        start_tile(posn_ref, 1 - slot)

    wait_tile(pos_ref, slot)

    wts = wts_ref[...]
    f = wts[:, 0:1] * buf_ref[slot, 0]
    for k in range(1, TOP_K):
        f = f + wts[:, k:k + 1] * buf_ref[slot, k]
    xn = _layer_norm(ALPHA * x_ref[...] + f, g_ref[...], b_ref[...])
    xo_ref[...] = xn
    xb_ref[...] = xn.astype(BF16)


def _combine(pos_tiles, wts, x, g, b, ys):
    t_pad, d = x.shape
    tm = ROW_TILE
    n = t_pad // tm
    smem_tile = lambda f: pl.BlockSpec((1, 1, tm * TOP_K), f, memory_space=pltpu.SMEM)
    return pl.pallas_call(
        _combine_kernel,
        grid=(n,),
        in_specs=[
            smem_tile(lambda i: (i, 0, 0)),
            smem_tile(lambda i: (jnp.minimum(i + 1, n - 1), 0, 0)),
            pl.BlockSpec((tm, LANES), lambda i: (i, 0)),
            pl.BlockSpec((tm, d), lambda i: (i, 0)),
            pl.BlockSpec((1, d), lambda i: (0, 0)),
            pl.BlockSpec((1, d), lambda i: (0, 0)),
            pl.BlockSpec(memory_space=pl.ANY),
        ],
        out_specs=[
            pl.BlockSpec((tm, d), lambda i: (i, 0)),
            pl.BlockSpec((tm, d), lambda i: (i, 0)),
        ],
        out_shape=[
            jax.ShapeDtypeStruct((t_pad, d), F32),
            jax.ShapeDtypeStruct((t_pad, d), BF16),
        ],
        scratch_shapes=[pltpu.VMEM((2, TOP_K, tm, d), F32), pltpu.SemaphoreType.DMA((2,))],
        compiler_params=_cparams("arbitrary"),
        name="moe_combine",
    )(pos_tiles, pos_tiles, wts, x, g, b, ys)


def _moe(xo, xpk, meta, wts, counts, ln_g, ln_b, w_gu, b_gu, w_dn, b_dn, layer, *, n_valid_rows):
    t_pad, d = xo.shape
    tm = ROW_TILE
    te = EXPERT_TILE
    n_tiles = -(-(n_valid_rows * TOP_K + N_EXPERTS * (te - 1)) // te)
    cnt = counts[0].astype(I32)
    padded = ((cnt + te - 1) // te) * te
    ends = jnp.cumsum(padded)
    starts = ends - padded
    ids = meta[:, :TOP_K]
    rank = meta[:, TOP_K:2 * TOP_K]
    valid = (jnp.arange(t_pad, dtype=I32) < n_valid_rows)[:, None]
    pos = jnp.where(valid, jnp.take(starts, ids) + rank, 0).astype(I32)
    pos_tiles = pos.reshape(t_pad // tm, 1, tm * TOP_K)
    n_valid_tiles = (ends[-1] // te).astype(I32)
    tile_idx = jnp.arange(n_tiles, dtype=I32)
    tile_e = jnp.minimum(jnp.sum((ends[None, :] <= (tile_idx * te)[:, None]).astype(I32), axis=1), N_EXPERTS - 1)
    tile_ok = tile_idx < n_valid_tiles
    tile_e = jnp.where(tile_ok, tile_e, tile_e[jnp.maximum(n_valid_tiles - 1, 0)])
    first = jnp.logical_and(tile_ok, tile_e != jnp.concatenate([jnp.full((1,), -1, I32), tile_e[:-1]])).astype(I32)
    run = jnp.cumsum(first) - 1
    eid = jnp.arange(N_EXPERTS, dtype=I32)
    nonempty = padded > 0
    later = jnp.logical_and(nonempty[None, :], eid[None, :] > eid[:, None])
    next_of = jnp.min(jnp.where(later, eid[None, :], N_EXPERTS), axis=1)
    next_of = jnp.where(next_of == N_EXPERTS, jnp.min(jnp.where(nonempty, eid, N_EXPERTS)), next_of)
    nxt = jnp.take(next_of, tile_e).astype(I32)
    sched = (tile_e, first, run.astype(I32), nxt, jnp.stack([n_valid_tiles, jnp.sum(first)]).astype(I32))

    xs = _dispatch(pos_tiles, xpk, jnp.zeros((n_tiles * te, d // 2), I32), n_valid_rows=n_valid_rows)
    h = _expert_gu(sched, xs, w_gu, b_gu, layer)
    ys = _expert_dn(sched, h, w_dn, b_dn, layer)
    return _combine(pos_tiles, wts, xo, ln_g, ln_b, ys)


def _kv_proj_kernel(x_ref, w_ref, kf_ref, vf_ref, kb_ref, vb_ref):
    kv = _dot(x_ref[...], w_ref[...])
    half = kv.shape[1] // 2
    k = kv[:, :half]
    v = kv[:, half:]
    kf_ref[...] = k
    vf_ref[...] = v
    kb_ref[...] = k.astype(BF16)
    vb_ref[...] = v.astype(BF16)


def _kv_proj(xb, w_kv):
    t_pad, d = xb.shape
    n = w_kv.shape[1] // 2
    tm = ROW_TILE
    row = lambda i: (i, 0)
    return pl.pallas_call(
        _kv_proj_kernel,
        grid=(t_pad // tm,),
        in_specs=[pl.BlockSpec((tm, d), row),
                  pl.BlockSpec((d, 2 * n), lambda i: (0, 0), pipeline_mode=pl.Buffered(1))],
        out_specs=[pl.BlockSpec((tm, n), row)] * 4,
        out_shape=[jax.ShapeDtypeStruct((t_pad, n), F32), jax.ShapeDtypeStruct((t_pad, n), F32),
                   jax.ShapeDtypeStruct((t_pad, n), BF16), jax.ShapeDtypeStruct((t_pad, n), BF16)],
        compiler_params=_cparams("arbitrary"),
        name="kv_proj",
    )(xb, w_kv)


def _q_proj_kernel(x_ref, w_ref, q_ref):
    q_ref[...] = (_dot(x_ref[...], w_ref[...]) * ATTN_SCALE).astype(BF16)


def _q_proj(xb, w_q):
    t_pad, d = xb.shape
    n = w_q.shape[1]
    tm = ROW_TILE
    return pl.pallas_call(
        _q_proj_kernel,
        grid=(t_pad // tm,),
        in_specs=[pl.BlockSpec((tm, d), lambda i: (i, 0)),
                  pl.BlockSpec((d, n), lambda i: (0, 0), pipeline_mode=pl.Buffered(1))],
        out_specs=pl.BlockSpec((tm, n), lambda i: (i, 0)),
        out_shape=jax.ShapeDtypeStruct((t_pad, n), BF16),
        compiler_params=_cparams("arbitrary"),
        name="q_proj",
    )(xb, w_q)


def _lambda_value(lam_ref, lam_init):
    e1 = jnp.exp(jnp.sum(lam_ref[0:1, :] * lam_ref[1:2, :], axis=1, keepdims=True))
    e2 = jnp.exp(jnp.sum(lam_ref[2:3, :] * lam_ref[3:4, :], axis=1, keepdims=True))
    return e1 - e2 + lam_init


def _bias_from_buckets(bucket, tab_ref, head):
    far = tab_ref[NUM_BUCKETS - 1, head]
    bias = jnp.where(bucket < 0, -jnp.inf, 0.0).astype(F32)
    for b in range(NUM_BUCKETS - 1):
        bias = jnp.where(bucket == b, tab_ref[b, head] - far, bias)
    return bias


def _diff_finalize(acc, l, lam, g, lam_init):
    n = acc.shape[0] // 2
    o = acc[:n] / l[:n] - lam * (acc[n:] / l[n:])
    o = o * lax.rsqrt(jnp.mean(o * o, axis=-1, keepdims=True) + RMS_EPS)
    return o * g * (1.0 - lam_init)


def _prompt_attn_kernel(tab_ref, q_ref, k_ref, v_ref, bkt_ref, lam_ref, g_ref, o_ref,
                        bias_ref, vt_ref, qst_ref, sa_ref, sb_ref, m_ref, l_ref, acc_ref, *, lam_init):
    h = pl.program_id(1)
    qi = pl.program_id(2)
    tq = q_ref.shape[0]
    n_kb, _, tk = vt_ref.shape
    n_near = bkt_ref.shape[0]

    @pl.when(qi == 0)
    def _():
        for t in range(n_near):
            bias_ref[t] = _bias_from_buckets(bkt_ref[t], tab_ref, h)
        bias_ref[n_near] = jnp.zeros(bias_ref.shape[1:], F32)
        for c in range(n_kb):
            vt_ref[c] = v_ref[c * tk:(c + 1) * tk, :].T.astype(BF16)

    q = q_ref[...].astype(F32)
    lane = lax.broadcasted_iota(I32, q.shape, 1)
    q0 = jnp.where(lane < HEAD_DIM, q, 0.0)
    q1 = jnp.where(lane >= HEAD_DIM, q, 0.0)
    qst_ref[...] = jnp.concatenate([q0.T, q1.T], axis=1).astype(BF16)
    m_ref[...] = jnp.full(m_ref.shape, -jnp.inf, F32)
    l_ref[...] = jnp.zeros(l_ref.shape, F32)
    acc_ref[...] = jnp.zeros(acc_ref.shape, F32)

    n_blocks = (qi + 1) * (tq // tk)

    def scores(ki, s_ref):
        start = pl.multiple_of(ki * tk, tk)
        s_ref[...] = _dot(k_ref[pl.ds(start, tk), :], qst_ref[...])

    def consume(ki, s_ref):
        bias = bias_ref[jnp.minimum(n_blocks - 1 - ki, n_near)]
        s = s_ref[...] + jnp.concatenate([bias, bias], axis=1)
        m_old = m_ref[...]
        m_new = jnp.maximum(m_old, jnp.max(s, axis=0, keepdims=True))
        p = jnp.exp(s - m_new)
        corr = jnp.exp(m_old - m_new)
        l_ref[...] = corr * l_ref[...] + jnp.sum(p, axis=0, keepdims=True)
        acc_ref[...] = corr * acc_ref[...] + _dot(vt_ref[ki], p.astype(BF16))
        m_ref[...] = m_new

    scores(0, sa_ref)

    def pair(j, carry):
        k0 = 2 * j
        scores(k0 + 1, sb_ref)
        consume(k0, sa_ref)
        scores(jnp.minimum(k0 + 2, n_blocks - 1), sa_ref)
        consume(k0 + 1, sb_ref)
        return carry

    lax.fori_loop(0, n_blocks // 2, pair, 0)

    @pl.when(n_blocks % 2 == 1)
    def _():
        consume(n_blocks - 1, sa_ref)

    lam = _lambda_value(lam_ref, lam_init)
    on = acc_ref[...] / l_ref[...]
    o = (on[:, :tq] - lam * on[:, tq:]).T
    o = o * lax.rsqrt(jnp.mean(o * o, axis=-1, keepdims=True) + RMS_EPS)
    o_ref[...] = (o * g_ref[...] * (1.0 - lam_init)).astype(o_ref.dtype)


def _prompt_attn(q, kb, vf, buckets_t, table, lam_vecs, g, *, batch, seq, lam_init):
    tq, tk = ATTN_Q_TILE, ATTN_K_TILE
    nq = seq // tq
    return pl.pallas_call(
        functools.partial(_prompt_attn_kernel, lam_init=lam_init),
        grid_spec=pltpu.PrefetchScalarGridSpec(
            num_scalar_prefetch=1,
            grid=(batch, N_HEADS, nq),
            in_specs=[
                pl.BlockSpec((tq, V_DIM), lambda b, h, i, tab: (b * nq + i, h)),
                pl.BlockSpec((seq, V_DIM), lambda b, h, i, tab: (b, h)),
                pl.BlockSpec((seq, V_DIM), lambda b, h, i, tab: (b, h)),
                pl.BlockSpec(buckets_t.shape, lambda b, h, i, tab: (0, 0, 0)),
                pl.BlockSpec((4, HEAD_DIM), lambda b, h, i, tab: (0, 0)),
                pl.BlockSpec((1, V_DIM), lambda b, h, i, tab: (0, 0)),
            ],
            out_specs=pl.BlockSpec((tq, V_DIM), lambda b, h, i, tab: (b * nq + i, h)),
            scratch_shapes=[
                pltpu.VMEM((buckets_t.shape[0] + 1, tk, tq), F32),
                pltpu.VMEM((seq // tk, V_DIM, tk), BF16),
                pltpu.VMEM((V_DIM, 2 * tq), BF16),
                pltpu.VMEM((tk, 2 * tq), F32),
                pltpu.VMEM((tk, 2 * tq), F32),
                pltpu.VMEM((1, 2 * tq), F32),
                pltpu.VMEM((1, 2 * tq), F32),
                pltpu.VMEM((V_DIM, 2 * tq), F32),
            ],
        ),
        out_shape=jax.ShapeDtypeStruct((batch * seq, N_HEADS * V_DIM), BF16),
        compiler_params=_cparams("arbitrary", "arbitrary", "arbitrary"),
        name="prompt_attn",
    )(table, q, kb, vf, buckets_t, lam_vecs, g)


def _sample_attn_kernel(pt_ref, tab_ref, q_ref, kn_ref, vn_ref, bkt_last_ref, bkt_new_ref, lam_ref, g_ref, *refs,
                        lam_init, n_pages_step):
    del pt_ref
    kt_refs = refs[:n_pages_step]
    v_refs = refs[n_pages_step:2 * n_pages_step]
    o_ref, qbd_ref, m_ref, l_ref, acc_ref, bias_ref = refs[2 * n_pages_step:]
    c = pl.program_id(1)
    nc = pl.num_programs(1)
    lq = q_ref.shape[1]
    rows_h = 2 * lq
    rows_g = DEC_HEAD_GROUP * rows_h
    feat_g = DEC_HEAD_GROUP * V_DIM
    page = kt_refs[0].shape[1]

    @pl.when(c == 0)
    def _():
        qbd_ref[...] = jnp.zeros(qbd_ref.shape, qbd_ref.dtype)
        for h in range(N_HEADS):
            q = q_ref[0, :, h * V_DIM:(h + 1) * V_DIM]
            lane = lax.broadcasted_iota(I32, q.shape, 1)
            zero = jnp.zeros_like(q)
            qs = jnp.concatenate([jnp.where(lane < HEAD_DIM, q, zero), jnp.where(lane >= HEAD_DIM, q, zero)], axis=0)
            qbd_ref[h * rows_h:(h + 1) * rows_h, h * V_DIM:(h + 1) * V_DIM] = qs.astype(BF16)
        m_ref[...] = jnp.full(m_ref.shape, -jnp.inf, F32)
        l_ref[...] = jnp.zeros(l_ref.shape, F32)
        acc_ref[...] = jnp.zeros(acc_ref.shape, F32)
        bias_ref[...] = jnp.zeros(bias_ref.shape, F32)

    def stacked_bias(bucket):
        tiles = []
        for h in range(N_HEADS):
            b = _bias_from_buckets(bucket, tab_ref, h)
            tiles += [b, b]
        return jnp.concatenate(tiles, axis=0)

    @pl.when(c == nc - 1)
    def _():
        bias_ref[...] = stacked_bias(bkt_last_ref[...])

    def update(s, vmat):
        m_old = m_ref[...]
        m_new = jnp.maximum(m_old, jnp.max(s, axis=1, keepdims=True))
        p = jnp.exp(s - m_new)
        corr = jnp.exp(m_old - m_new)
        l_ref[...] = corr * l_ref[...] + jnp.sum(p, axis=1, keepdims=True)
        pb = p.astype(BF16)
        tiles = []
        for gi in range(N_HEADS // DEC_HEAD_GROUP):
            r = _dot(pb[gi * rows_g:(gi + 1) * rows_g, :], vmat[:, gi * feat_g:(gi + 1) * feat_g])
            tiles += [r[h * rows_h:(h + 1) * rows_h, h * V_DIM:(h + 1) * V_DIM] for h in range(DEC_HEAD_GROUP)]
        pv = jnp.concatenate(tiles, axis=0)
        acc_ref[...] = corr * acc_ref[...] + pv
        m_ref[...] = m_new

    def group_logits(kmat, nt):
        parts = []
        for gi in range(N_HEADS // DEC_HEAD_GROUP):
            qg = qbd_ref[gi * rows_g:(gi + 1) * rows_g, gi * feat_g:(gi + 1) * feat_g]
            if nt:
                parts.append(_dot_nt(qg, kmat[:, gi * feat_g:(gi + 1) * feat_g]))
            else:
                parts.append(_dot(qg, kmat[gi * feat_g:(gi + 1) * feat_g, :]))
        return jnp.concatenate(parts, axis=0)

    s = jnp.concatenate([group_logits(kt[...].astype(BF16), False) for kt in kt_refs], axis=1) + bias_ref[...]
    vmat = jnp.concatenate(
        [jnp.concatenate([vr[pl.ds(h, page, stride=N_HEADS), :] for h in range(N_HEADS)], axis=1).astype(BF16)
         for vr in v_refs], axis=0)
    update(s, vmat)

    @pl.when(c == nc - 1)
    def _():
        update(group_logits(kn_ref[0], True) + stacked_bias(bkt_new_ref[...]), vn_ref[0])
        lam = _lambda_value(lam_ref, lam_init)
        for h in range(N_HEADS):
            rows = slice(h * rows_h, (h + 1) * rows_h)
            o = _diff_finalize(acc_ref[rows, :], l_ref[rows, :], lam, g_ref[...], lam_init)
            o_ref[0, :, h * V_DIM:(h + 1) * V_DIM] = o.astype(o_ref.dtype)


def _sample_attn(q_s, k_new, v_new, cache_kt, cache_vr, page_table, bkt_last, bkt_new, table, lam_vecs, g, *,
                 lam_init):
    bsz, lq, dm = q_s.shape
    page = cache_kt.shape[2]
    n_pages = page_table.shape[1]
    pg = DEC_PAGES_PER_STEP
    nc = n_pages // pg
    rows = N_HEADS * 2 * lq

    def page_spec(shape, r):
        return pl.BlockSpec((None,) + shape, lambda b, c, pt, tab: (pt[b, c * pg + r], 0, 0))

    per_b = lambda b, c, pt, tab: (b, 0, 0)
    const2 = lambda b, c, pt, tab: (0, 0)
    return pl.pallas_call(
        functools.partial(_sample_attn_kernel, lam_init=lam_init, n_pages_step=pg),
        grid_spec=pltpu.PrefetchScalarGridSpec(
            num_scalar_prefetch=2,
            grid=(bsz, nc),
            in_specs=[
                pl.BlockSpec((1, lq, dm), per_b),
                pl.BlockSpec((1, page, dm), per_b),
                pl.BlockSpec((1, page, dm), per_b),
                pl.BlockSpec((lq, pg * page), const2),
                pl.BlockSpec((lq, page), const2),
                pl.BlockSpec((4, HEAD_DIM), const2),
                pl.BlockSpec((1, V_DIM), const2),
            ] + [page_spec((dm, page), r) for r in range(pg)]
              + [page_spec((page * N_HEADS, V_DIM), r) for r in range(pg)],
            out_specs=pl.BlockSpec((1, lq, dm), per_b),
            scratch_shapes=[
                pltpu.VMEM((rows, dm), BF16),
                pltpu.VMEM((rows, 1), F32),
                pltpu.VMEM((rows, 1), F32),
                pltpu.VMEM((rows, V_DIM), F32),
                pltpu.VMEM((rows, pg * page), F32),
            ],
        ),
        out_shape=jax.ShapeDtypeStruct((bsz, lq, dm), F32),
        compiler_params=_cparams("arbitrary", "arbitrary"),
        name="sample_attn",
    )(page_table, table, q_s, k_new, v_new, bkt_last, bkt_new, lam_vecs, g, *([cache_kt] * pg), *([cache_vr] * pg))


def _t5_bucket(n):
    max_exact = NUM_BUCKETS // 2
    nf = jnp.maximum(n, 1).astype(F32)
    large = max_exact + (jnp.log(nf / max_exact) / math.log(MAX_DISTANCE / max_exact)
                         * (NUM_BUCKETS - max_exact)).astype(I32)
    large = jnp.minimum(large, NUM_BUCKETS - 1)
    return jnp.where(n < 0, -1, jnp.where(n < max_exact, n, large)).astype(I32)


def kernel(x_prompt, x_sample, state_conv, cache_k, cache_v, page_table, conv_w_in, conv_w, conv_w_out, attn_w_q,
           attn_w_kv, lambda_q1, lambda_k1, lambda_q2, lambda_k2, subln_g, attn_w_o, rel_bias, ln_mix_g, ln_mix_b,
           ln_ffn_g, ln_ffn_b, router_w, router_b, expert_w_gu, expert_b_gu, expert_w_dn, expert_b_dn):
    bp, lp, d = x_prompt.shape
    bs, ls, _ = x_sample.shape
    n_pool, page = cache_k.shape[0], cache_k.shape[1]
    n_pages = page_table.shape[1]
    tp, ts = bp * lp, bs * ls
    t_valid = tp + ts
    tm = ROW_TILE
    assert lp % tm == 0 and ts <= tm and lp % ATTN_Q_TILE == 0 and ATTN_Q_TILE % ATTN_K_TILE == 0 and ls >= CONV_WIDTH - 1
    assert n_pages % DEC_PAGES_PER_STEP == 0 and page >= ls
    n_prompt_tiles = tp // tm
    t_pad = tp + tm
    assert ATTN_K_TILE >= MAX_DISTANCE and DEC_PAGES_PER_STEP * page >= MAX_DISTANCE

    x = jnp.concatenate([x_prompt.reshape(tp, d), x_sample.reshape(ts, d), jnp.zeros((t_pad - t_valid, d), F32)], axis=0)
    xb = x.astype(BF16)

    tq, tk = ATTN_Q_TILE, ATTN_K_TILE
    rq = jnp.arange(tq, dtype=I32)[:, None]
    ck = jnp.arange(tk, dtype=I32)[None, :]
    bkt_prompt = jnp.stack([_t5_bucket(rq - ck - (tq - tk) + t * tk).T for t in range(tq // tk + 1)])
    rs = jnp.arange(ls, dtype=I32)[:, None]
    chunk = DEC_PAGES_PER_STEP * page
    bkt_last = _t5_bucket(chunk + rs - jnp.arange(chunk, dtype=I32)[None, :])
    bkt_new = _t5_bucket(rs - jnp.arange(page, dtype=I32)[None, :])

    st_prompt = jnp.zeros((bp, CONV_WIDTH - 1, d), F32)
    conv_states_p, conv_states_s = [], []
    k_f = v_f = k_b = v_b = None
    cache_kt = jnp.transpose(cache_k, (0, 2, 3, 4, 1)).reshape(n_pool, d, page)
    cache_vr = cache_v.reshape(n_pool, page * N_HEADS, V_DIM)

    for l in range(DEPTH):
        if l < N_CONV_LAYERS:
            z, st_p, st_s = _conv_in(xb, conv_w_in[l].astype(BF16), st_prompt, state_conv[l], conv_w[l],
                                     n_prompt_tiles=n_prompt_tiles, tiles_per_seq=lp // tm, dec_batch=bs, dec_seq=ls)
            conv_states_p.append(st_p)
            conv_states_s.append(st_s)
            w_out = conv_w_out[l].astype(BF16)
        else:
            j = l - N_CONV_LAYERS
            if j == 0:
                k_f, v_f, k_b, v_b = _kv_proj(xb, attn_w_kv.astype(BF16))
                k_new = jnp.zeros((bs, page, d), BF16).at[:, :ls].set(k_b[tp:t_valid].reshape(bs, ls, d))
                v_new = jnp.zeros((bs, page, d), BF16).at[:, :ls].set(v_b[tp:t_valid].reshape(bs, ls, d))
            lam_init = 0.8 - 0.6 * math.exp(-0.3 * l)
            lam_vecs = jnp.stack([lambda_q1[j], lambda_k1[j], lambda_q2[j], lambda_k2[j]])
            g = subln_g[j].reshape(1, V_DIM)
            q = _q_proj(xb, attn_w_q[j].astype(BF16))
            o_p = _prompt_attn(q, k_b, v_f, bkt_prompt, rel_bias, lam_vecs, g, batch=bp, seq=lp, lam_init=lam_init)
            o_s = _sample_attn(q[tp:t_valid].astype(F32).reshape(bs, ls, d), k_new, v_new, cache_kt, cache_vr,
                               page_table, bkt_last, bkt_new, rel_bias, lam_vecs, g, lam_init=lam_init)
            z = jnp.concatenate([o_p, o_s.astype(BF16).reshape(ts, d), jnp.zeros((t_pad - t_valid, d), BF16)], axis=0)
            w_out = attn_w_o[j].astype(BF16)

        wr = router_w[l]
        wr_hi = wr.astype(BF16)
        wr_lo = (wr - wr_hi.astype(F32)).astype(BF16)
        xo, xpk, meta, wts, counts = _proj_router(
            z, w_out, x, ln_mix_g[l].reshape(1, d), ln_mix_b[l].reshape(1, d), wr_hi, wr_lo,
            router_b[l].reshape(1, N_EXPERTS), n_valid_rows=t_valid)
        x, xb = _moe(xo, xpk, meta, wts, counts, ln_ffn_g[l].reshape(1, d), ln_ffn_b[l].reshape(1, d),
                     expert_w_gu, expert_b_gu.reshape(DEPTH, N_EXPERTS, 1, -1), expert_w_dn,
                     expert_b_dn.reshape(DEPTH, N_EXPERTS, 1, -1), l, n_valid_rows=t_valid)

    hk = (N_HEADS, 2, HEAD_DIM)
    hv = (N_HEADS, V_DIM)
    return (x[:tp].reshape(bp, lp, d), x[tp:t_valid].reshape(bs, ls, d), jnp.stack(conv_states_p),
            k_f[:tp].reshape(bp, lp, *hk), v_f[:tp].reshape(bp, lp, *hv), jnp.stack(conv_states_s),
            k_f[tp:t_valid].reshape(bs, ls, *hk), v_f[tp:t_valid].reshape(bs, ls, *hv))
```

```python
import functools
import math

import jax
import jax.numpy as jnp
from jax import lax
from jax.experimental import pallas as pl
from jax.experimental.pallas import tpu as pltpu

F32 = jnp.float32
BF16 = jnp.bfloat16
I32 = jnp.int32

DEPTH = 4
N_CONV_LAYERS = 2
CONV_WIDTH = 3
N_HEADS = 16
HEAD_DIM = 64
V_DIM = 128
NUM_BUCKETS = 32
MAX_DISTANCE = 128
N_EXPERTS = 32
TOP_K = 4
SWIGLU_LIMIT = 7.0
SWIGLU_ALPHA = 1.702
LN_EPS = 1e-5
RMS_EPS = 1e-5
ALPHA = (2 * DEPTH) ** 0.25
ATTN_SCALE = HEAD_DIM ** -0.5

ROW_TILE = 256
EXPERT_TILE = 256
FF_TILE_GU = 1024
FF_TILE_DN = 2048
CONV_COL_TILE = 1024
ATTN_Q_TILE = 512
ATTN_K_TILE = 256
DEC_PAGES_PER_STEP = 8
DEC_HEAD_GROUP = 4
VMEM_LIMIT = 56 * 1024 * 1024
LANES = 128
DMA_LOOP_UNROLL = 8


def _cparams(*sem):
    return pltpu.CompilerParams(dimension_semantics=sem, vmem_limit_bytes=VMEM_LIMIT)


def _dot(a, b):
    return jnp.dot(a, b, preferred_element_type=F32)


def _dot_nt(a, b):
    return lax.dot_general(a, b, (((1,), (1,)), ((), ())), preferred_element_type=F32)


def _layer_norm(r, g, b):
    mu = jnp.mean(r, axis=-1, keepdims=True)
    xc = r - mu
    var = jnp.mean(xc * xc, axis=-1, keepdims=True)
    return xc * lax.rsqrt(var + LN_EPS) * g + b


def _pack_bf16_pairs(xn):
    half = xn.shape[1] // 2
    hi = lax.bitcast_convert_type(xn[:, :half].astype(BF16).astype(F32), I32)
    lo = lax.bitcast_convert_type(xn[:, half:].astype(BF16).astype(F32), I32)
    return (hi & jnp.int32(-65536)) | lax.shift_right_logical(lo, jnp.int32(16))


def _unpack_bf16_pairs(pk):
    hi = lax.bitcast_convert_type(pk & jnp.int32(-65536), F32).astype(BF16)
    lo = lax.bitcast_convert_type(lax.shift_left(pk, jnp.int32(16)), F32).astype(BF16)
    return hi, lo


def _conv_in_kernel(x_ref, wb_ref, wc_ref, wh_ref, stp_ref, sts_ref, cw_ref,
                    z_ref, sop_ref, sos_ref, carry_ref, *, n_prompt_tiles, tiles_per_seq, dec_batch, dec_seq):
    i = pl.program_id(1)
    x = x_ref[...]
    bg = _dot(x, wb_ref[...])
    u = _dot(x, wc_ref[...]) * _dot(x, wh_ref[...])
    tm, tn = u.shape
    row = lax.broadcasted_iota(I32, (tm, tn), 0)
    s1 = pltpu.roll(u, 1, axis=0)
    s2 = pltpu.roll(u, 2, axis=0)
    cw = cw_ref[...]

    def finish(s1f, s2f):
        v = cw[0:1, :] * s2f + cw[1:2, :] * s1f + cw[2:3, :] * u
        z_ref[...] = (bg * v).astype(z_ref.dtype)

    @pl.when(i < n_prompt_tiles)
    def _():
        seq_start = (i % tiles_per_seq) == 0
        p1 = jnp.where(seq_start, stp_ref[0, 1:2, :], carry_ref[1:2, :])
        p2 = jnp.where(seq_start, stp_ref[0, 0:1, :], carry_ref[0:1, :])
        s1f = jnp.where(row == 0, p1, s1)
        s2f = jnp.where(row == 0, p2, jnp.where(row == 1, p1, s2))
        finish(s1f, s2f)
        tail = u[tm - 2:tm, :]
        carry_ref[0:2, :] = tail
        sop_ref[0] = tail

    @pl.when(i == n_prompt_tiles)
    def _():
        s1f, s2f = s1, s2
        for s in range(dec_batch):
            r0 = s * dec_seq
            p1 = sts_ref[s, 1:2, :]
            p2 = sts_ref[s, 0:1, :]
            s1f = jnp.where(row == r0, p1, s1f)
            s2f = jnp.where(row == r0, p2, jnp.where(row == r0 + 1, p1, s2f))
            sos_ref[s] = u[r0 + dec_seq - 2:r0 + dec_seq, :]
        finish(s1f, s2f)


def _conv_in(xb, w_in, st_prompt, st_sample, conv_w, *, n_prompt_tiles, tiles_per_seq, dec_batch, dec_seq):
    t_pad, d = xb.shape
    tm, tn = ROW_TILE, CONV_COL_TILE
    nb = d // tn
    n_seq_p = st_prompt.shape[0]
    grid = (nb, t_pad // tm)
    seq_of = lambda i: jnp.minimum(i // tiles_per_seq, n_seq_p - 1)
    kern = functools.partial(_conv_in_kernel, n_prompt_tiles=n_prompt_tiles, tiles_per_seq=tiles_per_seq,
                             dec_batch=dec_batch, dec_seq=dec_seq)
    return pl.pallas_call(
        kern,
        grid=grid,
        in_specs=[
            pl.BlockSpec((tm, d), lambda j, i: (i, 0)),
            pl.BlockSpec((d, tn), lambda j, i: (0, j)),
            pl.BlockSpec((d, tn), lambda j, i: (0, nb + j)),
            pl.BlockSpec((d, tn), lambda j, i: (0, 2 * nb + j)),
            pl.BlockSpec((1, CONV_WIDTH - 1, tn), lambda j, i: (seq_of(i), 0, j)),
            pl.BlockSpec((dec_batch, CONV_WIDTH - 1, tn), lambda j, i: (0, 0, j)),
            pl.BlockSpec((CONV_WIDTH, tn), lambda j, i: (0, j)),
        ],
        out_specs=[
            pl.BlockSpec((tm, tn), lambda j, i: (i, j)),
            pl.BlockSpec((1, CONV_WIDTH - 1, tn), lambda j, i: (seq_of(i), 0, j)),
            pl.BlockSpec((dec_batch, CONV_WIDTH - 1, tn), lambda j, i: (0, 0, j)),
        ],
        out_shape=[
            jax.ShapeDtypeStruct((t_pad, d), BF16),
            jax.ShapeDtypeStruct(st_prompt.shape, F32),
            jax.ShapeDtypeStruct(st_sample.shape, F32),
        ],
        scratch_shapes=[pltpu.VMEM((8, tn), F32)],
        compiler_params=_cparams("arbitrary", "arbitrary"),
        name="conv_in",
    )(xb, w_in, w_in, w_in, st_prompt, st_sample, conv_w)


def _proj_router_kernel(z_ref, w_ref, x_ref, g_ref, b_ref, wrh_ref, wrl_ref, br_ref,
                        xo_ref, xpk_ref, meta_ref, wts_ref, cnt_ref, carry_ref, *, n_valid_rows):
    i = pl.program_id(0)
    tm = z_ref.shape[0]

    @pl.when(i == 0)
    def _():
        carry_ref[...] = jnp.zeros_like(carry_ref)

    y = _dot(z_ref[...], w_ref[...])
    xn = _layer_norm(ALPHA * x_ref[...] + y, g_ref[...], b_ref[...])
    xo_ref[...] = xn
    xpk_ref[...] = _pack_bf16_pairs(xn)

    xh = xn.astype(BF16)
    xl = (xn - xh.astype(F32)).astype(BF16)
    wrh = wrh_ref[...]
    logits = _dot(xh, wrh) + (_dot(xl, wrh) + _dot(xh, wrl_ref[...])) + br_ref[...]

    ne = logits.shape[1]
    lane = lax.broadcasted_iota(I32, (tm, ne), 1)
    lane_f = lane.astype(F32)
    work = logits
    ids, vals = [], []
    for _ in range(TOP_K):
        m = jnp.max(work, axis=1, keepdims=True)
        idx = jnp.min(jnp.where(work == m, lane_f, float(ne)), axis=1, keepdims=True).astype(I32)
        ids.append(idx)
        vals.append(m)
        work = jnp.where(lane == idx, -jnp.inf, work)
    exps = [jnp.exp(v - vals[0]) for v in vals]
    denom = exps[0] + exps[1] + exps[2] + exps[3]

    grow = i * tm + lax.broadcasted_iota(I32, (tm, 1), 0)
    valid = grow < n_valid_rows
    onehots = [jnp.logical_and(lane == idx, valid) for idx in ids]
    sel = onehots[0] | onehots[1] | onehots[2] | onehots[3]
    selb = jnp.where(sel, 1.0, 0.0).astype(BF16)
    tri = (lax.broadcasted_iota(I32, (tm, tm), 0) >= lax.broadcasted_iota(I32, (tm, tm), 1)).astype(BF16)
    incl = _dot(tri, selb)
    excl = incl - selb.astype(F32) + carry_ref[...]

    lane128 = lax.broadcasted_iota(I32, (tm, LANES), 1)
    meta = jnp.zeros((tm, LANES), I32)
    wts = jnp.zeros((tm, LANES), F32)
    for k in range(TOP_K):
        rank = jnp.sum(jnp.where(onehots[k], excl, 0.0), axis=1, keepdims=True).astype(I32)
        meta = jnp.where(lane128 == k, ids[k], meta)
        meta = jnp.where(lane128 == TOP_K + k, rank, meta)
        wts = jnp.where(lane128 == k, jnp.where(valid, exps[k] / denom, 0.0), wts)
    meta_ref[...] = meta
    wts_ref[...] = wts
    new_cnt = carry_ref[...] + incl[tm - 1:tm, :]
    carry_ref[...] = new_cnt
    cnt_ref[...] = new_cnt


def _proj_router(z, w, x, g, b, wr_hi, wr_lo, br, *, n_valid_rows):
    t_pad, d = x.shape
    kdim = z.shape[1]
    tm = ROW_TILE
    ne = wr_hi.shape[1]
    const = lambda i: (0, 0)
    row = lambda i: (i, 0)
    return pl.pallas_call(
        functools.partial(_proj_router_kernel, n_valid_rows=n_valid_rows),
        grid=(t_pad // tm,),
        in_specs=[
            pl.BlockSpec((tm, kdim), row),
            pl.BlockSpec((kdim, d), const, pipeline_mode=pl.Buffered(1)),
            pl.BlockSpec((tm, d), row),
            pl.BlockSpec((1, d), const),
            pl.BlockSpec((1, d), const),
            pl.BlockSpec((d, ne), const),
            pl.BlockSpec((d, ne), const),
            pl.BlockSpec((1, ne), const),
        ],
        out_specs=[
            pl.BlockSpec((tm, d), row),
            pl.BlockSpec((tm, d // 2), row),
            pl.BlockSpec((tm, LANES), row),
            pl.BlockSpec((tm, LANES), row),
            pl.BlockSpec((1, ne), const),
        ],
        out_shape=[
            jax.ShapeDtypeStruct((t_pad, d), F32),
            jax.ShapeDtypeStruct((t_pad, d // 2), I32),
            jax.ShapeDtypeStruct((t_pad, LANES), I32),
            jax.ShapeDtypeStruct((t_pad, LANES), F32),
            jax.ShapeDtypeStruct((1, ne), F32),
        ],
        scratch_shapes=[pltpu.VMEM((1, ne), F32)],
        compiler_params=_cparams("arbitrary"),
        name="proj_router",
    )(z, w, x, g, b, wr_hi, wr_lo, br)


def _dispatch_kernel(pos_ref, xpk_ref, xs_in_ref, xs_ref, sem, *, n_valid_rows):
    del xs_in_ref
    i = pl.program_id(0)
    tm = xpk_ref.shape[0]
    n_full, n_rest = divmod(n_valid_rows, tm)

    def row_copy(r, k):
        p = pos_ref[0, 0, r * TOP_K + k]
        return pltpu.make_async_copy(xpk_ref.at[pl.ds(r, 1), :], xs_ref.at[pl.ds(p, 1), :], sem)

    def start(r, c):
        for k in range(TOP_K):
            row_copy(r, k).start(priority=k % 2)
        return c

    def wait(r, c):
        for k in range(TOP_K):
            row_copy(r, k).wait()
        return c

    def scatter_rows(n_rows):
        lax.fori_loop(0, n_rows, start, 0, unroll=DMA_LOOP_UNROLL)
        lax.fori_loop(0, n_rows, wait, 0, unroll=DMA_LOOP_UNROLL)

    @pl.when(i < n_full)
    def _():
        scatter_rows(tm)

    if n_rest:
        @pl.when(i == n_full)
        def _():
            scatter_rows(n_rest)


def _dispatch(pos_tiles, xpk, xs_init, *, n_valid_rows):
    t_pad, dh = xpk.shape
    tm = ROW_TILE
    return pl.pallas_call(
        functools.partial(_dispatch_kernel, n_valid_rows=n_valid_rows),
        grid=(t_pad // tm,),
        in_specs=[
            pl.BlockSpec((1, 1, tm * TOP_K), lambda i: (i, 0, 0), memory_space=pltpu.SMEM),
            pl.BlockSpec((tm, dh), lambda i: (i, 0)),
            pl.BlockSpec(memory_space=pl.ANY),
        ],
        out_specs=pl.BlockSpec(memory_space=pl.ANY),
        out_shape=jax.ShapeDtypeStruct(xs_init.shape, xs_init.dtype),
        scratch_shapes=[pltpu.SemaphoreType.DMA(())],
        input_output_aliases={2: 0},
        compiler_params=_cparams("arbitrary"),
        name="moe_dispatch",
    )(pos_tiles, xpk, xs_init)


def _run_weights(te_ref, first_ref, run_ref, nxt_ref, nv_ref, copies, n_col_passes):
    j = pl.program_id(0)
    i = pl.program_id(1)
    n_runs = nv_ref[1]
    g = j * n_runs + run_ref[i]
    slot = g % 2

    @pl.when(first_ref[i] == 1)
    def _():
        @pl.when(g == 0)
        def _():
            for c in copies(j, te_ref[i], slot):
                c.start()

        for c in copies(j, te_ref[i], slot):
            c.wait()
        last_run = run_ref[i] == n_runs - 1

        @pl.when(jnp.logical_not(jnp.logical_and(last_run, j == n_col_passes - 1)))
        def _():
            for c in copies(jnp.where(last_run, j + 1, j), nxt_ref[i], 1 - slot):
                c.start()

    return slot


def _expert_gu_kernel(te_ref, first_ref, run_ref, nxt_ref, nv_ref, xs_ref, bg_ref, bu_ref, w_ref, h_ref, wbuf, sem,
                      *, layer):
    i = pl.program_id(1)
    nj = pl.num_programs(0)
    tn = wbuf.shape[-1]

    def copies(jj, e, s):
        return [pltpu.make_async_copy(w_ref.at[layer, e, :, pl.ds(pl.multiple_of((n * nj + jj) * tn, tn), tn)],
                                      wbuf.at[s, n], sem.at[s, n]) for n in range(2)]

    @pl.when(i < nv_ref[0])
    def _():
        slot = _run_weights(te_ref, first_ref, run_ref, nxt_ref, nv_ref, copies, nj)
        xhi, xlo = _unpack_bf16_pairs(xs_ref[...])
        half = xhi.shape[1]
        wg = wbuf[slot, 0].astype(BF16)
        wu = wbuf[slot, 1].astype(BF16)
        g = _dot(xhi, wg[:half]) + _dot(xlo, wg[half:]) + bg_ref[...]
        u = _dot(xhi, wu[:half]) + _dot(xlo, wu[half:]) + bu_ref[...]
        g = jnp.minimum(g, SWIGLU_LIMIT)
        u = jnp.clip(u, -SWIGLU_LIMIT, SWIGLU_LIMIT)
        h_ref[...] = ((u + 1.0) * (g * jax.nn.sigmoid(SWIGLU_ALPHA * g))).astype(h_ref.dtype)

    @pl.when(i >= nv_ref[0])
    def _():
        h_ref[...] = jnp.zeros_like(h_ref)


def _expert_gu(sched, xs, w_gu, b_gu, layer):
    p_rows, dh = xs.shape
    d = 2 * dh
    dff = w_gu.shape[3] // 2
    tm, tn = EXPERT_TILE, FF_TILE_GU
    nj = dff // tn
    tile = lambda j, i, te, fi, ru, nx, nv: jnp.minimum(i, nv[0] - 1)
    return pl.pallas_call(
        functools.partial(_expert_gu_kernel, layer=layer),
        grid_spec=pltpu.PrefetchScalarGridSpec(
            num_scalar_prefetch=5,
            grid=(nj, p_rows // tm),
            in_specs=[
                pl.BlockSpec((tm, dh), lambda j, i, te, fi, ru, nx, nv: (tile(j, i, te, fi, ru, nx, nv), 0)),
                pl.BlockSpec((None, None, 1, tn), lambda j, i, te, fi, ru, nx, nv: (layer, te[i], 0, j)),
                pl.BlockSpec((None, None, 1, tn), lambda j, i, te, fi, ru, nx, nv: (layer, te[i], 0, nj + j)),
                pl.BlockSpec(memory_space=pl.ANY),
            ],
            out_specs=pl.BlockSpec((tm, tn), lambda j, i, te, fi, ru, nx, nv: (i, j)),
            scratch_shapes=[pltpu.VMEM((2, 2, d, tn), F32), pltpu.SemaphoreType.DMA((2, 2))],
        ),
        out_shape=jax.ShapeDtypeStruct((p_rows, dff), BF16),
        compiler_params=_cparams("arbitrary", "arbitrary"),
        name="expert_gu",
    )(*sched, xs, b_gu, b_gu, w_gu)


def _expert_dn_kernel(te_ref, first_ref, run_ref, nxt_ref, nv_ref, h_ref, b_ref, w_ref, y_ref, wbuf, sem, *, layer):
    i = pl.program_id(1)
    nj = pl.num_programs(0)
    tn = wbuf.shape[-1]

    def copies(jj, e, s):
        return [pltpu.make_async_copy(w_ref.at[layer, e, :, pl.ds(pl.multiple_of(jj * tn, tn), tn)],
                                      wbuf.at[s], sem.at[s])]

    @pl.when(i < nv_ref[0])
    def _():
        slot = _run_weights(te_ref, first_ref, run_ref, nxt_ref, nv_ref, copies, nj)
        y_ref[...] = _dot(h_ref[...], wbuf[slot].astype(BF16)) + b_ref[...]

    @pl.when(i >= nv_ref[0])
    def _():
        y_ref[...] = jnp.zeros_like(y_ref)


def _expert_dn(sched, h, w_dn, b_dn, layer):
    p_rows, dff = h.shape
    d = w_dn.shape[3]
    tm, tn = EXPERT_TILE, FF_TILE_DN
    nj = d // tn
    tile = lambda j, i, te, fi, ru, nx, nv: jnp.minimum(i, nv[0] - 1)
    return pl.pallas_call(
        functools.partial(_expert_dn_kernel, layer=layer),
        grid_spec=pltpu.PrefetchScalarGridSpec(
            num_scalar_prefetch=5,
            grid=(nj, p_rows // tm),
            in_specs=[
                pl.BlockSpec((tm, dff), lambda j, i, te, fi, ru, nx, nv: (tile(j, i, te, fi, ru, nx, nv), 0)),
                pl.BlockSpec((None, None, 1, tn), lambda j, i, te, fi, ru, nx, nv: (layer, te[i], 0, j)),
                pl.BlockSpec(memory_space=pl.ANY),
            ],
            out_specs=pl.BlockSpec((tm, tn), lambda j, i, te, fi, ru, nx, nv: (i, j)),
            scratch_shapes=[pltpu.VMEM((2, dff, tn), F32), pltpu.SemaphoreType.DMA((2,))],
        ),
        out_shape=jax.ShapeDtypeStruct((p_rows, d), F32),
        compiler_params=_cparams("arbitrary", "arbitrary"),
        name="expert_dn",
    )(*sched, h, b_dn, w_dn)


def _combine_kernel(pos_ref, posn_ref, wts_ref, x_ref, g_ref, b_ref, ys_ref, xo_ref, xb_ref, buf_ref, sem):
    i = pl.program_id(0)
    n = pl.num_programs(0)
    tm = x_ref.shape[0]

    def row_copy(p_ref, slot, r, k):
        p = p_ref[0, 0, r * TOP_K + k]
        return pltpu.make_async_copy(ys_ref.at[pl.ds(p, 1), :], buf_ref.at[slot, k, pl.ds(r, 1), :], sem.at[slot])

    def start_tile(p_ref, slot):
        def body(r, c):
            for k in range(TOP_K):
                row_copy(p_ref, slot, r, k).start(priority=k % 2)
            return c
        lax.fori_loop(0, tm, body, 0, unroll=DMA_LOOP_UNROLL)

    def wait_tile(p_ref, slot):
        def body(r, c):
            for k in range(TOP_K):
                row_copy(p_ref, slot, r, k).wait()
            return c
        lax.fori_loop(0, tm, body, 0, unroll=DMA_LOOP_UNROLL)

    slot = i % 2

    @pl.when(i == 0)
    def _():
        start_tile(pos_ref, 0)

    @pl.when(i + 1 < n)
    def _():
        start_tile(posn_ref, 1 - slot)

    wait_tile(pos_ref, slot)

    wts = wts_ref[...]
    f = wts[:, 0:1] * buf_ref[slot, 0]
    for k in range(1, TOP_K):
        f = f + wts[:, k:k + 1] * buf_ref[slot, k]
    xn = _layer_norm(ALPHA * x_ref[...] + f, g_ref[...], b_ref[...])
    xo_ref[...] = xn
    xb_ref[...] = xn.astype(BF16)


def _combine(pos_tiles, wts, x, g, b, ys):
    t_pad, d = x.shape
    tm = ROW_TILE
    n = t_pad // tm
    smem_tile = lambda f: pl.BlockSpec((1, 1, tm * TOP_K), f, memory_space=pltpu.SMEM)
    return pl.pallas_call(
        _combine_kernel,
        grid=(n,),
        in_specs=[
            smem_tile(lambda i: (i, 0, 0)),
            smem_tile(lambda i: (jnp.minimum(i + 1, n - 1), 0, 0)),
            pl.BlockSpec((tm, LANES), lambda i: (i, 0)),
            pl.BlockSpec((tm, d), lambda i: (i, 0)),
            pl.BlockSpec((1, d), lambda i: (0, 0)),
            pl.BlockSpec((1, d), lambda i: (0, 0)),
            pl.BlockSpec(memory_space=pl.ANY),
        ],
        out_specs=[
            pl.BlockSpec((tm, d), lambda i: (i, 0)),
            pl.BlockSpec((tm, d), lambda i: (i, 0)),
        ],
        out_shape=[
            jax.ShapeDtypeStruct((t_pad, d), F32),
            jax.ShapeDtypeStruct((t_pad, d), BF16),
        ],
        scratch_shapes=[pltpu.VMEM((2, TOP_K, tm, d), F32), pltpu.SemaphoreType.DMA((2,))],
        compiler_params=_cparams("arbitrary"),
        name="moe_combine",
    )(pos_tiles, pos_tiles, wts, x, g, b, ys)


def _moe(xo, xpk, meta, wts, counts, ln_g, ln_b, w_gu, b_gu, w_dn, b_dn, layer, xs_buf, *, n_valid_rows):
    t_pad, d = xo.shape
    tm = ROW_TILE
    te = EXPERT_TILE
    n_tiles = xs_buf.shape[0] // te
    cnt = counts[0].astype(I32)
    padded = ((cnt + te - 1) // te) * te
    ends = jnp.cumsum(padded)
    starts = ends - padded
    ids = meta[:, :TOP_K]
    rank = meta[:, TOP_K:2 * TOP_K]
    valid = (jnp.arange(t_pad, dtype=I32) < n_valid_rows)[:, None]
    pos = jnp.where(valid, jnp.take(starts, ids) + rank, 0).astype(I32)
    pos_tiles = pos.reshape(t_pad // tm, 1, tm * TOP_K)
    n_valid_tiles = (ends[-1] // te).astype(I32)
    tile_idx = jnp.arange(n_tiles, dtype=I32)
    tile_e = jnp.minimum(jnp.sum((ends[None, :] <= (tile_idx * te)[:, None]).astype(I32), axis=1), N_EXPERTS - 1)
    tile_ok = tile_idx < n_valid_tiles
    tile_e = jnp.where(tile_ok, tile_e, tile_e[jnp.maximum(n_valid_tiles - 1, 0)])
    first = jnp.logical_and(tile_ok, tile_e != jnp.concatenate([jnp.full((1,), -1, I32), tile_e[:-1]])).astype(I32)
    run = jnp.cumsum(first) - 1
    eid = jnp.arange(N_EXPERTS, dtype=I32)
    nonempty = padded > 0
    later = jnp.logical_and(nonempty[None, :], eid[None, :] > eid[:, None])
    next_of = jnp.min(jnp.where(later, eid[None, :], N_EXPERTS), axis=1)
    next_of = jnp.where(next_of == N_EXPERTS, jnp.min(jnp.where(nonempty, eid, N_EXPERTS)), next_of)
    nxt = jnp.take(next_of, tile_e).astype(I32)
    sched = (tile_e, first, run.astype(I32), nxt, jnp.stack([n_valid_tiles, jnp.sum(first)]).astype(I32))

    xs = _dispatch(pos_tiles, xpk, xs_buf, n_valid_rows=n_valid_rows)
    h = _expert_gu(sched, xs, w_gu, b_gu, layer)
    ys = _expert_dn(sched, h, w_dn, b_dn, layer)
    return _combine(pos_tiles, wts, xo, ln_g, ln_b, ys) + (xs,)


def _kv_proj_kernel(x_ref, w_ref, kf_ref, vf_ref, kb_ref, vb_ref):
    kv = _dot(x_ref[...], w_ref[...])
    half = kv.shape[1] // 2
    k = kv[:, :half]
    v = kv[:, half:]
    kf_ref[...] = k
    vf_ref[...] = v
    kb_ref[...] = k.astype(BF16)
    vb_ref[...] = v.astype(BF16)


def _kv_proj(xb, w_kv):
    t_pad, d = xb.shape
    n = w_kv.shape[1] // 2
    tm = ROW_TILE
    row = lambda i: (i, 0)
    return pl.pallas_call(
        _kv_proj_kernel,
        grid=(t_pad // tm,),
        in_specs=[pl.BlockSpec((tm, d), row),
                  pl.BlockSpec((d, 2 * n), lambda i: (0, 0), pipeline_mode=pl.Buffered(1))],
        out_specs=[pl.BlockSpec((tm, n), row)] * 4,
        out_shape=[jax.ShapeDtypeStruct((t_pad, n), F32), jax.ShapeDtypeStruct((t_pad, n), F32),
                   jax.ShapeDtypeStruct((t_pad, n), BF16), jax.ShapeDtypeStruct((t_pad, n), BF16)],
        compiler_params=_cparams("arbitrary"),
        name="kv_proj",
    )(xb, w_kv)


def _q_proj_kernel(x_ref, w_ref, q_ref):
    q_ref[...] = (_dot(x_ref[...], w_ref[...]) * ATTN_SCALE).astype(BF16)


def _q_proj(xb, w_q):
    t_pad, d = xb.shape
    n = w_q.shape[1]
    tm = ROW_TILE
    return pl.pallas_call(
        _q_proj_kernel,
        grid=(t_pad // tm,),
        in_specs=[pl.BlockSpec((tm, d), lambda i: (i, 0)),
                  pl.BlockSpec((d, n), lambda i: (0, 0), pipeline_mode=pl.Buffered(1))],
        out_specs=pl.BlockSpec((tm, n), lambda i: (i, 0)),
        out_shape=jax.ShapeDtypeStruct((t_pad, n), BF16),
        compiler_params=_cparams("arbitrary"),
        name="q_proj",
    )(xb, w_q)


def _lambda_value(lam_ref, lam_init):
    e1 = jnp.exp(jnp.sum(lam_ref[0:1, :] * lam_ref[1:2, :], axis=1, keepdims=True))
    e2 = jnp.exp(jnp.sum(lam_ref[2:3, :] * lam_ref[3:4, :], axis=1, keepdims=True))
    return e1 - e2 + lam_init


def _bias_from_buckets(bucket, tab_ref, head):
    far = tab_ref[NUM_BUCKETS - 1, head]
    bias = jnp.where(bucket < 0, -jnp.inf, 0.0).astype(F32)
    for b in range(NUM_BUCKETS - 1):
        bias = jnp.where(bucket == b, tab_ref[b, head] - far, bias)
    return bias


def _diff_finalize(acc, l, lam, g, lam_init):
    n = acc.shape[0] // 2
    o = acc[:n] / l[:n] - lam * (acc[n:] / l[n:])
    o = o * lax.rsqrt(jnp.mean(o * o, axis=-1, keepdims=True) + RMS_EPS)
    return o * g * (1.0 - lam_init)


def _bias_tiles_kernel(tab_ref, bkt_ref, o_ref):
    h = pl.program_id(0)
    n_near = bkt_ref.shape[0]
    for t in range(n_near):
        o_ref[t] = _bias_from_buckets(bkt_ref[t], tab_ref, h)
    o_ref[n_near] = jnp.zeros(o_ref.shape[1:], F32)


def _bias_tiles(table, buckets_t):
    n_near, tk, tq = buckets_t.shape
    return pl.pallas_call(
        _bias_tiles_kernel,
        grid_spec=pltpu.PrefetchScalarGridSpec(
            num_scalar_prefetch=1,
            grid=(N_HEADS,),
            in_specs=[pl.BlockSpec(buckets_t.shape, lambda h, tab: (0, 0, 0))],
            out_specs=pl.BlockSpec((None, n_near + 1, tk, tq), lambda h, tab: (h, 0, 0, 0)),
        ),
        out_shape=jax.ShapeDtypeStruct((N_HEADS, n_near + 1, tk, tq), F32),
        compiler_params=_cparams("arbitrary"),
        name="bias_tiles",
    )(table, buckets_t)


def _prompt_attn_kernel(q_ref, k_ref, v_ref, bias_ref, lam_ref, g_ref, o_ref,
                        vt_ref, qst_ref, sa_ref, sb_ref, m_ref, l_ref, acc_ref, *, lam_init):
    qi = pl.program_id(2)
    tq = q_ref.shape[0]
    n_kb, _, tk = vt_ref.shape
    n_near = bias_ref.shape[0] - 1

    @pl.when(qi == 0)
    def _():
        for c in range(n_kb):
            vt_ref[c] = v_ref[c * tk:(c + 1) * tk, :].T.astype(BF16)

    q = q_ref[...].astype(F32)
    lane = lax.broadcasted_iota(I32, q.shape, 1)
    q0 = jnp.where(lane < HEAD_DIM, q, 0.0)
    q1 = jnp.where(lane >= HEAD_DIM, q, 0.0)
    qst_ref[...] = jnp.concatenate([q0.T, q1.T], axis=1).astype(BF16)
    m_ref[...] = jnp.full(m_ref.shape, -jnp.inf, F32)
    l_ref[...] = jnp.zeros(l_ref.shape, F32)
    acc_ref[...] = jnp.zeros(acc_ref.shape, F32)

    n_blocks = (qi + 1) * (tq // tk)

    def scores(ki, s_ref):
        start = pl.multiple_of(ki * tk, tk)
        s_ref[...] = _dot(k_ref[pl.ds(start, tk), :], qst_ref[...])

    def consume(ki, s_ref):
        bias = bias_ref[jnp.minimum(n_blocks - 1 - ki, n_near)]
        s = s_ref[...] + jnp.concatenate([bias, bias], axis=1)
        m_old = m_ref[...]
        m_new = jnp.maximum(m_old, jnp.max(s, axis=0, keepdims=True))
        p = jnp.exp(s - m_new)
        corr = jnp.exp(m_old - m_new)
        l_ref[...] = corr * l_ref[...] + jnp.sum(p, axis=0, keepdims=True)
        acc_ref[...] = corr * acc_ref[...] + _dot(vt_ref[ki], p.astype(BF16))
        m_ref[...] = m_new

    scores(0, sa_ref)

    def pair(j, carry):
        k0 = 2 * j
        scores(k0 + 1, sb_ref)
        consume(k0, sa_ref)
        scores(jnp.minimum(k0 + 2, n_blocks - 1), sa_ref)
        consume(k0 + 1, sb_ref)
        return carry

    lax.fori_loop(0, n_blocks // 2, pair, 0)

    @pl.when(n_blocks % 2 == 1)
    def _():
        consume(n_blocks - 1, sa_ref)

    lam = _lambda_value(lam_ref, lam_init)
    on = acc_ref[...] / l_ref[...]
    o = (on[:, :tq] - lam * on[:, tq:]).T
    o = o * lax.rsqrt(jnp.mean(o * o, axis=-1, keepdims=True) + RMS_EPS)
    o_ref[...] = (o * g_ref[...] * (1.0 - lam_init)).astype(o_ref.dtype)


def _prompt_attn(q, kb, vf, bias_tiles, lam_vecs, g, *, batch, seq, lam_init):
    tq, tk = ATTN_Q_TILE, ATTN_K_TILE
    nq = seq // tq
    return pl.pallas_call(
        functools.partial(_prompt_attn_kernel, lam_init=lam_init),
        grid_spec=pltpu.PrefetchScalarGridSpec(
            num_scalar_prefetch=0,
            grid=(batch, N_HEADS, nq),
            in_specs=[
                pl.BlockSpec((tq, V_DIM), lambda b, h, i: (b * nq + i, h)),
                pl.BlockSpec((seq, V_DIM), lambda b, h, i: (b, h)),
                pl.BlockSpec((seq, V_DIM), lambda b, h, i: (b, h)),
                pl.BlockSpec((None,) + bias_tiles.shape[1:], lambda b, h, i: (h, 0, 0, 0)),
                pl.BlockSpec((4, HEAD_DIM), lambda b, h, i: (0, 0)),
                pl.BlockSpec((1, V_DIM), lambda b, h, i: (0, 0)),
            ],
            out_specs=pl.BlockSpec((tq, V_DIM), lambda b, h, i: (b * nq + i, h)),
            scratch_shapes=[
                pltpu.VMEM((seq // tk, V_DIM, tk), BF16),
                pltpu.VMEM((V_DIM, 2 * tq), BF16),
                pltpu.VMEM((tk, 2 * tq), F32),
                pltpu.VMEM((tk, 2 * tq), F32),
                pltpu.VMEM((1, 2 * tq), F32),
                pltpu.VMEM((1, 2 * tq), F32),
                pltpu.VMEM((V_DIM, 2 * tq), F32),
            ],
        ),
        out_shape=jax.ShapeDtypeStruct((batch * seq, N_HEADS * V_DIM), BF16),
        compiler_params=_cparams("arbitrary", "arbitrary", "arbitrary"),
        name="prompt_attn",
    )(q, kb, vf, bias_tiles, lam_vecs, g)


def _sample_attn_kernel(pt_ref, tab_ref, q_ref, kn_ref, vn_ref, bkt_last_ref, bkt_new_ref, lam_ref, g_ref, *refs,
                        lam_init, n_pages_step):
    del pt_ref
    kt_refs = refs[:n_pages_step]
    v_refs = refs[n_pages_step:2 * n_pages_step]
    o_ref, qbd_ref, m_ref, l_ref, acc_ref, bias_ref = refs[2 * n_pages_step:]
    c = pl.program_id(1)
    nc = pl.num_programs(1)
    lq = q_ref.shape[1]
    rows_h = 2 * lq
    rows_g = DEC_HEAD_GROUP * rows_h
    feat_g = DEC_HEAD_GROUP * V_DIM
    page = kt_refs[0].shape[1]

    @pl.when(c == 0)
    def _():
        qbd_ref[...] = jnp.zeros(qbd_ref.shape, qbd_ref.dtype)
        for h in range(N_HEADS):
            q = q_ref[0, :, h * V_DIM:(h + 1) * V_DIM]
            lane = lax.broadcasted_iota(I32, q.shape, 1)
            zero = jnp.zeros_like(q)
            qs = jnp.concatenate([jnp.where(lane < HEAD_DIM, q, zero), jnp.where(lane >= HEAD_DIM, q, zero)], axis=0)
            qbd_ref[h * rows_h:(h + 1) * rows_h, h * V_DIM:(h + 1) * V_DIM] = qs.astype(BF16)
        m_ref[...] = jnp.full(m_ref.shape, -jnp.inf, F32)
        l_ref[...] = jnp.zeros(l_ref.shape, F32)
        acc_ref[...] = jnp.zeros(acc_ref.shape, F32)
        bias_ref[...] = jnp.zeros(bias_ref.shape, F32)

    def stacked_bias(bucket):
        tiles = []
        for h in range(N_HEADS):
            b = _bias_from_buckets(bucket, tab_ref, h)
            tiles += [b, b]
        return jnp.concatenate(tiles, axis=0)

    @pl.when(c == nc - 1)
    def _():
        bias_ref[...] = stacked_bias(bkt_last_ref[...])

    def update(s, vmat):
        m_old = m_ref[...]
        m_new = jnp.maximum(m_old, jnp.max(s, axis=1, keepdims=True))
        p = jnp.exp(s - m_new)
        corr = jnp.exp(m_old - m_new)
        l_ref[...] = corr * l_ref[...] + jnp.sum(p, axis=1, keepdims=True)
        pb = p.astype(BF16)
        tiles = []
        for gi in range(N_HEADS // DEC_HEAD_GROUP):
            r = _dot(pb[gi * rows_g:(gi + 1) * rows_g, :], vmat[:, gi * feat_g:(gi + 1) * feat_g])
            tiles += [r[h * rows_h:(h + 1) * rows_h, h * V_DIM:(h + 1) * V_DIM] for h in range(DEC_HEAD_GROUP)]
        pv = jnp.concatenate(tiles, axis=0)
        acc_ref[...] = corr * acc_ref[...] + pv
        m_ref[...] = m_new

    def group_logits(kmat, nt):
        parts = []
        for gi in range(N_HEADS // DEC_HEAD_GROUP):
            qg = qbd_ref[gi * rows_g:(gi + 1) * rows_g, gi * feat_g:(gi + 1) * feat_g]
            if nt:
                parts.append(_dot_nt(qg, kmat[:, gi * feat_g:(gi + 1) * feat_g]))
            else:
                parts.append(_dot(qg, kmat[gi * feat_g:(gi + 1) * feat_g, :]))
        return jnp.concatenate(parts, axis=0)

    s = jnp.concatenate([group_logits(kt[...].astype(BF16), False) for kt in kt_refs], axis=1) + bias_ref[...]
    vmat = jnp.concatenate(
        [jnp.concatenate([vr[pl.ds(h, page, stride=N_HEADS), :] for h in range(N_HEADS)], axis=1).astype(BF16)
         for vr in v_refs], axis=0)
    update(s, vmat)

    @pl.when(c == nc - 1)
    def _():
        update(group_logits(kn_ref[0], True) + stacked_bias(bkt_new_ref[...]), vn_ref[0])
        lam = _lambda_value(lam_ref, lam_init)
        for h in range(N_HEADS):
            rows = slice(h * rows_h, (h + 1) * rows_h)
            o = _diff_finalize(acc_ref[rows, :], l_ref[rows, :], lam, g_ref[...], lam_init)
            o_ref[0, :, h * V_DIM:(h + 1) * V_DIM] = o.astype(o_ref.dtype)


def _sample_attn(q_s, k_new, v_new, cache_kt, cache_vr, page_table, bkt_last, bkt_new, table, lam_vecs, g, *,
                 lam_init):
    bsz, lq, dm = q_s.shape
    page = cache_kt.shape[2]
    n_pages = page_table.shape[1]
    pg = DEC_PAGES_PER_STEP
    nc = n_pages // pg
    rows = N_HEADS * 2 * lq

    def page_spec(shape, r):
        return pl.BlockSpec((None,) + shape, lambda b, c, pt, tab: (pt[b, c * pg + r], 0, 0))

    per_b = lambda b, c, pt, tab: (b, 0, 0)
    const2 = lambda b, c, pt, tab: (0, 0)
    return pl.pallas_call(
        functools.partial(_sample_attn_kernel, lam_init=lam_init, n_pages_step=pg),
        grid_spec=pltpu.PrefetchScalarGridSpec(
            num_scalar_prefetch=2,
            grid=(bsz, nc),
            in_specs=[
                pl.BlockSpec((1, lq, dm), per_b),
                pl.BlockSpec((1, page, dm), per_b),
                pl.BlockSpec((1, page, dm), per_b),
                pl.BlockSpec((lq, pg * page), const2),
                pl.BlockSpec((lq, page), const2),
                pl.BlockSpec((4, HEAD_DIM), const2),
                pl.BlockSpec((1, V_DIM), const2),
            ] + [page_spec((dm, page), r) for r in range(pg)]
              + [page_spec((page * N_HEADS, V_DIM), r) for r in range(pg)],
            out_specs=pl.BlockSpec((1, lq, dm), per_b),
            scratch_shapes=[
                pltpu.VMEM((rows, dm), BF16),
                pltpu.VMEM((rows, 1), F32),
                pltpu.VMEM((rows, 1), F32),
                pltpu.VMEM((rows, V_DIM), F32),
                pltpu.VMEM((rows, pg * page), F32),
            ],
        ),
        out_shape=jax.ShapeDtypeStruct((bsz, lq, dm), F32),
        compiler_params=_cparams("arbitrary", "arbitrary"),
        name="sample_attn",
    )(page_table, table, q_s, k_new, v_new, bkt_last, bkt_new, lam_vecs, g, *([cache_kt] * pg), *([cache_vr] * pg))


def _t5_bucket(n):
    max_exact = NUM_BUCKETS // 2
    nf = jnp.maximum(n, 1).astype(F32)
    large = max_exact + (jnp.log(nf / max_exact) / math.log(MAX_DISTANCE / max_exact)
                         * (NUM_BUCKETS - max_exact)).astype(I32)
    large = jnp.minimum(large, NUM_BUCKETS - 1)
    return jnp.where(n < 0, -1, jnp.where(n < max_exact, n, large)).astype(I32)


def kernel(x_prompt, x_sample, state_conv, cache_k, cache_v, page_table, conv_w_in, conv_w, conv_w_out, attn_w_q,
           attn_w_kv, lambda_q1, lambda_k1, lambda_q2, lambda_k2, subln_g, attn_w_o, rel_bias, ln_mix_g, ln_mix_b,
           ln_ffn_g, ln_ffn_b, router_w, router_b, expert_w_gu, expert_b_gu, expert_w_dn, expert_b_dn):
    bp, lp, d = x_prompt.shape
    bs, ls, _ = x_sample.shape
    n_pool, page = cache_k.shape[0], cache_k.shape[1]
    n_pages = page_table.shape[1]
    tp, ts = bp * lp, bs * ls
    t_valid = tp + ts
    tm = ROW_TILE
    assert lp % tm == 0 and ts <= tm and lp % ATTN_Q_TILE == 0 and ATTN_Q_TILE % ATTN_K_TILE == 0 and ls >= CONV_WIDTH - 1
    assert n_pages % DEC_PAGES_PER_STEP == 0 and page >= ls
    n_prompt_tiles = tp // tm
    t_pad = tp + tm
    assert ATTN_K_TILE >= MAX_DISTANCE and DEC_PAGES_PER_STEP * page >= MAX_DISTANCE

    x = jnp.concatenate([x_prompt.reshape(tp, d), x_sample.reshape(ts, d), jnp.zeros((t_pad - t_valid, d), F32)], axis=0)
    xb = x.astype(BF16)

    tq, tk = ATTN_Q_TILE, ATTN_K_TILE
    rq = jnp.arange(tq, dtype=I32)[:, None]
    ck = jnp.arange(tk, dtype=I32)[None, :]
    bkt_prompt = jnp.stack([_t5_bucket(rq - ck - (tq - tk) + t * tk).T for t in range(tq // tk + 1)])
    rs = jnp.arange(ls, dtype=I32)[:, None]
    chunk = DEC_PAGES_PER_STEP * page
    bkt_last = _t5_bucket(chunk + rs - jnp.arange(chunk, dtype=I32)[None, :])
    bkt_new = _t5_bucket(rs - jnp.arange(page, dtype=I32)[None, :])

    st_prompt = jnp.zeros((bp, CONV_WIDTH - 1, d), F32)
    n_sorted_tiles = -(-(t_valid * TOP_K + N_EXPERTS * (EXPERT_TILE - 1)) // EXPERT_TILE)
    xs_buf = jnp.zeros((n_sorted_tiles * EXPERT_TILE, d // 2), I32)
    conv_states_p, conv_states_s = [], []
    k_f = v_f = k_b = v_b = bias_p = None
    cache_kt = jnp.transpose(cache_k, (0, 2, 3, 4, 1)).reshape(n_pool, d, page)
    cache_vr = cache_v.reshape(n_pool, page * N_HEADS, V_DIM)

    for l in range(DEPTH):
        if l < N_CONV_LAYERS:
            z, st_p, st_s = _conv_in(xb, conv_w_in[l].astype(BF16), st_prompt, state_conv[l], conv_w[l],
                                     n_prompt_tiles=n_prompt_tiles, tiles_per_seq=lp // tm, dec_batch=bs, dec_seq=ls)
            conv_states_p.append(st_p)
            conv_states_s.append(st_s)
            w_out = conv_w_out[l].astype(BF16)
        else:
            j = l - N_CONV_LAYERS
            if j == 0:
                k_f, v_f, k_b, v_b = _kv_proj(xb, attn_w_kv.astype(BF16))
                bias_p = _bias_tiles(rel_bias, bkt_prompt)
                k_new = jnp.zeros((bs, page, d), BF16).at[:, :ls].set(k_b[tp:t_valid].reshape(bs, ls, d))
                v_new = jnp.zeros((bs, page, d), BF16).at[:, :ls].set(v_b[tp:t_valid].reshape(bs, ls, d))
            lam_init = 0.8 - 0.6 * math.exp(-0.3 * l)
            lam_vecs = jnp.stack([lambda_q1[j], lambda_k1[j], lambda_q2[j], lambda_k2[j]])
            g = subln_g[j].reshape(1, V_DIM)
            q = _q_proj(xb, attn_w_q[j].astype(BF16))
            o_p = _prompt_attn(q, k_b, v_f, bias_p, lam_vecs, g, batch=bp, seq=lp, lam_init=lam_init)
            o_s = _sample_attn(q[tp:t_valid].astype(F32).reshape(bs, ls, d), k_new, v_new, cache_kt, cache_vr,
                               page_table, bkt_last, bkt_new, rel_bias, lam_vecs, g, lam_init=lam_init)
            z = jnp.concatenate([o_p, o_s.astype(BF16).reshape(ts, d), jnp.zeros((t_pad - t_valid, d), BF16)], axis=0)
            w_out = attn_w_o[j].astype(BF16)

        wr = router_w[l]
        wr_hi = wr.astype(BF16)
        wr_lo = (wr - wr_hi.astype(F32)).astype(BF16)
        xo, xpk, meta, wts, counts = _proj_router(
            z, w_out, x, ln_mix_g[l].reshape(1, d), ln_mix_b[l].reshape(1, d), wr_hi, wr_lo,
            router_b[l].reshape(1, N_EXPERTS), n_valid_rows=t_valid)
        x, xb, xs_buf = _moe(xo, xpk, meta, wts, counts, ln_ffn_g[l].reshape(1, d), ln_ffn_b[l].reshape(1, d),
                     expert_w_gu, expert_b_gu.reshape(DEPTH, N_EXPERTS, 1, -1), expert_w_dn,
                     expert_b_dn.reshape(DEPTH, N_EXPERTS, 1, -1), l, xs_buf, n_valid_rows=t_valid)

    hk = (N_HEADS, 2, HEAD_DIM)
    hv = (N_HEADS, V_DIM)
    return (x[:tp].reshape(bp, lp, d), x[tp:t_valid].reshape(bs, ls, d), jnp.stack(conv_states_p),
            k_f[:tp].reshape(bp, lp, *hk), v_f[:tp].reshape(bp, lp, *hv), jnp.stack(conv_states_s),
            k_f[tp:t_valid].reshape(bs, ls, *hk), v_f[tp:t_valid].reshape(bs, ls, *hv))
```
